```python
import jax, jax.numpy as jnp
from jax import lax
import numpy as np

D_MODEL = 1024
BATCH = 2
SEQ = 16384
DEPTH = 1
DEC_BATCH = 8
DEC_SEQ = 8192
PAST_LEN = 128

REC_WIDTH = 512
ATT_WIDTH = D_MODEL - REC_WIDTH
REC_HEADS = 4
REC_DK = REC_WIDTH // REC_HEADS
REC_CHUNK = 64
ATT_HEADS = 8
ATT_KV_HEADS = 2
ATT_HEAD_DIM = ATT_WIDTH // ATT_HEADS
ATT_GROUP = ATT_HEADS // ATT_KV_HEADS
WINDOW = 128
ATT_BLOCK = 128
ROT_DIM = ATT_HEAD_DIM // 4
ROPE_THETA = 500000.0
N_EXPERTS = 16
EXPERT_FF = 2048
CAPACITY_FACTOR = 2
NORM_EPS = 1e-5
DEEPNORM_ALPHA = (2 * DEPTH) ** 0.25
DEEPNORM_BETA = (8 * DEPTH) ** -0.25
PROJ_SIZES = [REC_WIDTH, REC_WIDTH, REC_WIDTH, REC_WIDTH, REC_WIDTH,
              ATT_WIDTH, ATT_KV_HEADS * ATT_HEAD_DIM, ATT_KV_HEADS * ATT_HEAD_DIM]
IN_PROJ_WIDTH = sum(PROJ_SIZES)
PROJ_CUTS = [int(c) for c in np.cumsum(PROJ_SIZES)[:-1]]

kernel_name = "hymba_hgrn2_swa_ec_moe_deepnorm_encoder"


def layer_norm(x, g, b):
    xf = x.astype(jnp.float32)
    mu = xf.mean(-1, keepdims=True)
    var = jnp.square(xf - mu).mean(-1, keepdims=True)
    return ((xf - mu) * lax.rsqrt(var + NORM_EPS)).astype(x.dtype) * g + b


def rms_norm(x, g):
    xf = x.astype(jnp.float32)
    return (xf * lax.rsqrt(jnp.square(xf).mean(-1, keepdims=True) + NORM_EPS)).astype(x.dtype) * g


def lower_bound(lb_param, layer):
    return jnp.cumsum(jax.nn.softmax(lb_param.astype(jnp.float32), axis=0), axis=0)[layer]


def rotary(x, positions):
    half = ROT_DIM // 2
    inv = ROPE_THETA ** (-jnp.arange(half, dtype=jnp.float32) / half)
    ang = positions.astype(jnp.float32)[:, None] * inv[None, :]
    cos = jnp.cos(ang)[None, :, None, :]
    sin = jnp.sin(ang)[None, :, None, :]
    xr = x[..., :ROT_DIM].astype(jnp.float32)
    x1, x2 = xr[..., :half], xr[..., half:]
    rot = jnp.concatenate([x1 * cos - x2 * sin, x2 * cos + x1 * sin], axis=-1)
    return jnp.concatenate([rot.astype(x.dtype), x[..., ROT_DIM:]], axis=-1)


def hgrn_chunk_scan(q, k, v, log_f):
    dtype = v.dtype
    B, S, H, DK = q.shape
    DV = v.shape[-1]
    N = S // REC_CHUNK

    def chunks(t):
        return t.astype(jnp.float32).reshape(B, N, REC_CHUNK, H, t.shape[-1])

    q, k, v, log_f = chunks(q), chunks(k), chunks(v), chunks(log_f)
    b = jnp.cumsum(log_f, axis=2)
    b_mid = b[:, :, REC_CHUNK // 2 - 1:REC_CHUNK // 2]
    b_last = b[:, :, REC_CHUNK - 1:]
    a = jnp.einsum('bnthk,bnshk->bnhts', q * jnp.exp(b - b_mid), k * jnp.exp(b_mid - b))
    a = jnp.where(jnp.tril(jnp.ones((REC_CHUNK, REC_CHUNK), dtype=bool)), a, 0.0)
    o_intra = jnp.einsum('bnhts,bnshv->bnthv', a, v)
    u = jnp.einsum('bnshk,bnshv->nbhkv', k * jnp.exp(b_last - b), v)
    decay = jnp.moveaxis(jnp.exp(b_last[:, :, 0]), 1, 0)

    def step(state, inp):
        d, un = inp
        return d[..., None] * state + un, state

    _, s_prev = lax.scan(step, jnp.zeros((B, H, DK, DV), jnp.float32), (decay, u))
    o_inter = jnp.einsum('bnthk,nbhkv->bnthv', q * jnp.exp(b), s_prev)
    return (o_intra + o_inter).reshape(B, S, H, DV).astype(dtype)


def window_attention(q, k, v, sink):
    S = q.shape[0]
    N = S // ATT_BLOCK
    qb = q.reshape(N, ATT_BLOCK, ATT_KV_HEADS, ATT_GROUP, ATT_HEAD_DIM)

    def band(t):
        tp = jnp.pad(t, ((ATT_BLOCK, ATT_BLOCK), (0, 0), (0, 0)))
        tp = tp.reshape(N + 2, ATT_BLOCK, ATT_KV_HEADS, ATT_HEAD_DIM)
        return jnp.concatenate([tp[:-2], tp[1:-1], tp[2:]], axis=1)

    kb, vb = band(k), band(v)
    s = jnp.einsum('nqhgd,nkhd->nhgqk', qb, kb).astype(jnp.float32) * (ATT_HEAD_DIM ** -0.5)
    qi = jnp.arange(ATT_BLOCK)[:, None]
    kj = jnp.arange(3 * ATT_BLOCK)[None, :]
    kpos = jnp.arange(N)[:, None, None] * ATT_BLOCK - ATT_BLOCK + kj
    valid = (jnp.abs(kj - ATT_BLOCK - qi) <= WINDOW)[None] & (kpos >= 0) & (kpos < S)
    s = jnp.where(valid[:, None, None], s, -1e30)
    sink_b = sink.astype(jnp.float32)[None, :, :, None, None]
    m = jnp.maximum(s.max(-1, keepdims=True), sink_b)
    p = jnp.exp(s - m)
    probs = p / (p.sum(-1, keepdims=True) + jnp.exp(sink_b - m))
    o = jnp.einsum('nhgqk,nkhd->nqhgd', probs.astype(v.dtype), vb)
    return o.reshape(S, ATT_WIDTH)


def expert_choice_ffn(x, w_router, w_gate, w_up, w_down):
    T, D = x.shape
    cap = CAPACITY_FACTOR * T // N_EXPERTS
    aff = jax.nn.softmax(jnp.dot(x, w_router).astype(jnp.float32), axis=-1)
    gate_vals, tok_idx = lax.top_k(aff.T, cap)

    def expert(args):
        idx, wg, wu, wd = args
        xe = jnp.take(x, idx, axis=0)
        return (jax.nn.silu(xe @ wg) * (xe @ wu)) @ wd

    ye = lax.map(expert, (tok_idx, w_gate, w_up, w_down))
    ye = ye * gate_vals[..., None].astype(ye.dtype)
    return jnp.zeros_like(x).at[tok_idx.reshape(-1)].add(ye.reshape(-1, D))


def trunk(x, emb_ln_g, emb_ln_b, w_in, hgrn_lb_fwd, hgrn_lb_bwd, hgrn_norm_g, attn_sink, w_out,
          ln1_g, ln1_b, w_router, w_gate, w_up, w_down, ln2_g, ln2_b):
    x = layer_norm(x, emb_ln_g, emb_ln_b)
    B, S, D = x.shape
    pos = jnp.arange(S)
    flip = lambda t: t[:, ::-1]
    for l in range(DEPTH):
        proj = jnp.einsum('bsd,de->bse', x, w_in[l])
        q_r, zf_f, zf_b, i_r, g_r, q_a, k_a, v_a = jnp.split(proj, PROJ_CUTS, axis=-1)

        rh = lambda t: t.reshape(B, S, REC_HEADS, REC_DK)
        lb_f = lower_bound(hgrn_lb_fwd, l)
        lb_b = lower_bound(hgrn_lb_bwd, l)
        f_f = lb_f + (1.0 - lb_f) * jax.nn.sigmoid(zf_f.astype(jnp.float32))
        f_b = lb_b + (1.0 - lb_b) * jax.nn.sigmoid(zf_b.astype(jnp.float32))
        qh = rh(jax.nn.silu(q_r))
        vh = rh(i_r)
        o_fwd = hgrn_chunk_scan(qh, rh(1.0 - f_f), vh, rh(jnp.log(f_f)))
        o_bwd = hgrn_chunk_scan(flip(qh), flip(rh(1.0 - f_b)), flip(vh), flip(rh(jnp.log(f_b))))
        o_rec = o_fwd + flip(o_bwd)
        rec = (rms_norm(o_rec, hgrn_norm_g[l]) * rh(jax.nn.silu(g_r))).reshape(B, S, REC_WIDTH)

        qa = rotary(q_a.reshape(B, S, ATT_HEADS, ATT_HEAD_DIM), pos)
        qa = qa.reshape(B, S, ATT_KV_HEADS, ATT_GROUP, ATT_HEAD_DIM)
        ka = rotary(k_a.reshape(B, S, ATT_KV_HEADS, ATT_HEAD_DIM), pos)
        va = v_a.reshape(B, S, ATT_KV_HEADS, ATT_HEAD_DIM)
        sink = attn_sink[l].reshape(ATT_KV_HEADS, ATT_GROUP)
        att = lax.map(lambda qkv: window_attention(qkv[0], qkv[1], qkv[2], sink), (qa, ka, va))

        mix = jnp.einsum('bse,ed->bsd', jnp.concatenate([rec, att], axis=-1), w_out[l])
        x = layer_norm(DEEPNORM_ALPHA * x + mix, ln1_g[l], ln1_b[l])

        moe = expert_choice_ffn(x.reshape(B * S, D), w_router[l], w_gate[l], w_up[l], w_down[l])
        x = layer_norm(DEEPNORM_ALPHA * x + moe.reshape(B, S, D), ln2_g[l], ln2_b[l])
    return x


def setup_inputs(seed: int = 0) -> dict:
    key = jax.random.key(seed)
    ks = jax.random.split(key, 20)
    f32 = jnp.float32

    def nrm(k, shape, scale):
        return jax.random.normal(k, shape, f32) * scale

    x_prompt = nrm(ks[0], (BATCH, SEQ, D_MODEL), 1.0)
    x_sample = nrm(ks[1], (DEC_BATCH, DEC_SEQ, D_MODEL), 1.0)
    emb_ln_g = 1.0 + nrm(ks[2], (D_MODEL,), 0.02)
    emb_ln_b = nrm(ks[3], (D_MODEL,), 0.02)
    col_scale = jnp.concatenate([
        jnp.ones((4 * REC_WIDTH,), f32).at[3 * REC_WIDTH:].set(DEEPNORM_BETA),
        jnp.ones((REC_WIDTH + ATT_WIDTH + ATT_KV_HEADS * ATT_HEAD_DIM,), f32),
        jnp.full((ATT_KV_HEADS * ATT_HEAD_DIM,), DEEPNORM_BETA, f32)])
    w_in = nrm(ks[4], (DEPTH, D_MODEL, IN_PROJ_WIDTH), D_MODEL ** -0.5) * col_scale
    hgrn_lb_fwd = nrm(ks[5], (DEPTH + 1, REC_WIDTH), 0.1)
    hgrn_lb_bwd = nrm(ks[6], (DEPTH + 1, REC_WIDTH), 0.1)
    hgrn_norm_g = 1.0 + nrm(ks[7], (DEPTH, REC_DK), 0.02)
    attn_sink = nrm(ks[8], (DEPTH, ATT_HEADS), 0.5)
    w_out = nrm(ks[9], (DEPTH, D_MODEL, D_MODEL), D_MODEL ** -0.5) * DEEPNORM_BETA
    ln1_g = 1.0 + nrm(ks[10], (DEPTH, D_MODEL), 0.02)
    ln1_b = nrm(ks[11], (DEPTH, D_MODEL), 0.02)
    w_router = nrm(ks[12], (DEPTH, D_MODEL, N_EXPERTS), D_MODEL ** -0.5)
    w_gate = nrm(ks[13], (DEPTH, N_EXPERTS, D_MODEL, EXPERT_FF), D_MODEL ** -0.5)
    w_up = nrm(ks[14], (DEPTH, N_EXPERTS, D_MODEL, EXPERT_FF), D_MODEL ** -0.5)
    w_down = nrm(ks[15], (DEPTH, N_EXPERTS, EXPERT_FF, D_MODEL), EXPERT_FF ** -0.5) * DEEPNORM_BETA
    ln2_g = 1.0 + nrm(ks[16], (DEPTH, D_MODEL), 0.02)
    ln2_b = nrm(ks[17], (DEPTH, D_MODEL), 0.02)
    return {"x_prompt": x_prompt, "x_sample": x_sample, "emb_ln_g": emb_ln_g, "emb_ln_b": emb_ln_b,
            "w_in": w_in, "hgrn_lb_fwd": hgrn_lb_fwd, "hgrn_lb_bwd": hgrn_lb_bwd,
            "hgrn_norm_g": hgrn_norm_g, "attn_sink": attn_sink, "w_out": w_out,
            "ln1_g": ln1_g, "ln1_b": ln1_b, "w_router": w_router, "w_gate": w_gate, "w_up": w_up,
            "w_down": w_down, "ln2_g": ln2_g, "ln2_b": ln2_b}


def reference(x_prompt, x_sample, emb_ln_g, emb_ln_b, w_in, hgrn_lb_fwd, hgrn_lb_bwd, hgrn_norm_g,
              attn_sink, w_out, ln1_g, ln1_b, w_router, w_gate, w_up, w_down, ln2_g, ln2_b):
    y_prompt = trunk(x_prompt, emb_ln_g, emb_ln_b, w_in, hgrn_lb_fwd, hgrn_lb_bwd, hgrn_norm_g,
                     attn_sink, w_out, ln1_g, ln1_b, w_router, w_gate, w_up, w_down, ln2_g, ln2_b)
    y_sample = trunk(x_sample, emb_ln_g, emb_ln_b, w_in, hgrn_lb_fwd, hgrn_lb_bwd, hgrn_norm_g,
                     attn_sink, w_out, ln1_g, ln1_b, w_router, w_gate, w_up, w_down, ln2_g, ln2_b)
    return (y_prompt, y_sample)
```

```python
import functools

import jax
import jax.numpy as jnp
import numpy as np
from jax import lax
from jax.experimental import pallas as pl
from jax.experimental.pallas import tpu as pltpu

D_MODEL = 1024
REC_WIDTH = 512
ATT_WIDTH = 512
REC_HEADS = 4
REC_DK = 128
REC_CHUNK = 64
ATT_HEADS = 8
ATT_KV_HEADS = 2
ATT_HEAD_DIM = 64
ATT_GROUP = 4
WINDOW = 128
ATT_BLOCK = 128
ROT_DIM = 16
ROPE_THETA = 500000.0
N_EXPERTS = 16
EXPERT_FF = 2048
CAPACITY_FACTOR = 2
NORM_EPS = 1e-5
DEEPNORM_ALPHA = 2.0 ** 0.25
KV_WIDTH = ATT_KV_HEADS * ATT_HEAD_DIM
IN_PROJ_WIDTH = 5 * REC_WIDTH + ATT_WIDTH + 2 * KV_WIDTH

TOKEN_TILE = 512
SEQ_TILE = 512
DISPATCH_TILE = 256
FF_CHUNK = 512
VMEM_LIMIT_BYTES = 56 * 1024 * 1024

F32 = jnp.float32
BF16 = jnp.bfloat16
HIGHEST = lax.Precision.HIGHEST


def _params(n_axes):
    return pltpu.CompilerParams(dimension_semantics=("arbitrary",) * n_axes,
                                vmem_limit_bytes=VMEM_LIMIT_BYTES)


def _resident(shape):
    nd = len(shape)
    return pl.BlockSpec(shape, lambda *_: (0,) * nd, pipeline_mode=pl.Buffered(1))


def _layer_norm(x, g, b):
    mu = jnp.mean(x, axis=-1, keepdims=True)
    xc = x - mu
    var = jnp.mean(xc * xc, axis=-1, keepdims=True)
    return xc * lax.rsqrt(var + NORM_EPS) * g + b


def _lower_bound(p):
    p0, p1 = p[0:1, :], p[1:2, :]
    m = jnp.maximum(p0, p1)
    e0, e1 = jnp.exp(p0 - m), jnp.exp(p1 - m)
    return e0 / (e0 + e1)


def _dot(a, b, dims=((1,), (0,)), precision=None):
    return lax.dot_general(a, b, (dims, ((), ())), preferred_element_type=F32, precision=precision)


def _in_proj_kernel(x_ref, g_ref, b_ref, w_ref, lbf_ref, lbb_ref, rc_ref, rsa_ref, rsb_ref,
                    qs_ref, ff_ref, fb_ref, v_ref, gs_ref, qa_ref, ka_ref, va_ref):
    xn = _layer_norm(x_ref[...], g_ref[...], b_ref[...])
    proj = _dot(xn.astype(BF16), w_ref[...])
    W = REC_WIDTH
    q_r, zf, zb = proj[:, 0:W], proj[:, W:2 * W], proj[:, 2 * W:3 * W]
    i_r, g_r = proj[:, 3 * W:4 * W], proj[:, 4 * W:5 * W]
    q_a = proj[:, 5 * W:5 * W + ATT_WIDTH]
    k_a = proj[:, 5 * W + ATT_WIDTH:5 * W + ATT_WIDTH + KV_WIDTH]
    v_a = proj[:, 5 * W + ATT_WIDTH + KV_WIDTH:]
    lbf = _lower_bound(lbf_ref[...])
    lbb = _lower_bound(lbb_ref[...])
    qs_ref[...] = q_r * jax.nn.sigmoid(q_r)
    ff_ref[...] = lbf + (1.0 - lbf) * jax.nn.sigmoid(zf)
    fb_ref[...] = lbb + (1.0 - lbb) * jax.nn.sigmoid(zb)
    v_ref[...] = i_r.astype(BF16)
    gs_ref[...] = g_r * jax.nn.sigmoid(g_r)
    rc, rsa, rsb = rc_ref[...], rsa_ref[...], rsb_ref[...]
    k_rot = k_a * rc + pltpu.roll(k_a, KV_WIDTH - ROT_DIM // 2, 1) * rsa + pltpu.roll(k_a, ROT_DIM // 2, 1) * rsb
    ka_ref[...] = k_rot.astype(BF16)
    n_rep = ATT_WIDTH // KV_WIDTH
    rc4 = jnp.concatenate([rc] * n_rep, axis=1)
    rsa4 = jnp.concatenate([rsa] * n_rep, axis=1)
    rsb4 = jnp.concatenate([rsb] * n_rep, axis=1)
    q_rot = q_a * rc4 + pltpu.roll(q_a, ATT_WIDTH - ROT_DIM // 2, 1) * rsa4 + pltpu.roll(q_a, ROT_DIM // 2, 1) * rsb4
    qa_ref[...] = (q_rot * (ATT_HEAD_DIM ** -0.5)).astype(BF16)
    va_ref[...] = v_a.astype(BF16)


def _in_proj(x2d, g, b, w_in_bf16, lbf, lbb, rc, rsa, rsb, seq_len):
    T = x2d.shape[0]
    tm = TOKEN_TILE
    assert T % tm == 0 and seq_len % tm == 0
    n_seq = seq_len // tm
    row = lambda i: (i, 0)
    pos_row = lambda i: (i % n_seq, 0)
    wide = lambda w: pl.BlockSpec((tm, w), row)
    out_shapes = (
        jax.ShapeDtypeStruct((T, REC_WIDTH), F32),
        jax.ShapeDtypeStruct((T, REC_WIDTH), F32),
        jax.ShapeDtypeStruct((T, REC_WIDTH), F32),
        jax.ShapeDtypeStruct((T, REC_WIDTH), BF16),
        jax.ShapeDtypeStruct((T, REC_WIDTH), F32),
        jax.ShapeDtypeStruct((T, ATT_WIDTH), BF16),
        jax.ShapeDtypeStruct((T, KV_WIDTH), BF16),
        jax.ShapeDtypeStruct((T, KV_WIDTH), BF16),
    )
    return pl.pallas_call(
        _in_proj_kernel,
        grid=(T // tm,),
        in_specs=[wide(D_MODEL), _resident((1, D_MODEL)), _resident((1, D_MODEL)),
                  _resident((D_MODEL, IN_PROJ_WIDTH)), _resident((2, REC_WIDTH)), _resident((2, REC_WIDTH)),
                  pl.BlockSpec((tm, KV_WIDTH), pos_row), pl.BlockSpec((tm, KV_WIDTH), pos_row),
                  pl.BlockSpec((tm, KV_WIDTH), pos_row)],
        out_specs=[wide(REC_WIDTH)] * 5 + [wide(ATT_WIDTH), wide(KV_WIDTH), wide(KV_WIDTH)],
        out_shape=out_shapes,
        compiler_params=_params(1),
        name="in_proj",
    )(x2d, g, b, w_in_bf16, lbf, lbb, rc, rsa, rsb)


def _hgrn_kernel(q_ref, f_ref, v_ref, o_ref, state_ref, *, reverse):
    C = REC_CHUNK
    n_chunks = q_ref.shape[0] // C

    @pl.when(pl.program_id(1) == 0)
    def _():
        state_ref[...] = jnp.zeros_like(state_ref)

    t_i = lax.broadcasted_iota(jnp.int32, (C, C), 0)
    s_i = lax.broadcasted_iota(jnp.int32, (C, C), 1)
    causal = (s_i >= t_i) if reverse else (s_i <= t_i)
    csum = causal.astype(F32)
    mid = C // 2 if reverse else C // 2 - 1
    last = 0 if reverse else C - 1

    def chunk(ci, carry):
        c = (n_chunks - 1 - ci) if reverse else ci
        r0 = pl.multiple_of(c * C, C)
        f = f_ref[pl.ds(r0, C), :]
        q = q_ref[pl.ds(r0, C), :]
        v = v_ref[pl.ds(r0, C), :]
        kk = 1.0 - f
        b = _dot(csum, jnp.log(f), precision=HIGHEST)
        b_mid = b[mid:mid + 1, :]
        b_last = b[last:last + 1, :]
        qe = (q * jnp.exp(b - b_mid)).astype(BF16)
        ke = (kk * jnp.exp(b_mid - b)).astype(BF16)
        ku = (kk * jnp.exp(b_last - b)).astype(BF16)
        qb = (q * jnp.exp(b)).astype(BF16)
        decay = jnp.exp(b_last)
        outs = []
        for h in range(REC_HEADS):
            sl = slice(h * REC_DK, (h + 1) * REC_DK)
            a = _dot(qe[:, sl], ke[:, sl], ((1,), (1,)))
            a = jnp.where(causal, a, 0.0).astype(BF16)
            st = state_ref[h]
            o = _dot(a, v[:, sl]) + _dot(qb[:, sl], st.astype(BF16), ((1,), (1,)))
            u_t = _dot(v[:, sl], ku[:, sl], ((0,), (0,)))
            state_ref[h] = st * decay[:, sl] + u_t
            outs.append(o)
        o_ref[pl.ds(r0, C), :] = jnp.concatenate(outs, axis=1)
        return carry

    lax.fori_loop(0, n_chunks, chunk, 0)


def _hgrn(qs, f, v, batch, seq_len, reverse):
    T = qs.shape[0]
    tc = SEQ_TILE
    assert seq_len % tc == 0
    n_s = seq_len // tc
    if reverse:
        idx = lambda b, s: (b * n_s + (n_s - 1 - s), 0)
    else:
        idx = lambda b, s: (b * n_s + s, 0)
    spec = pl.BlockSpec((tc, REC_WIDTH), idx)
    return pl.pallas_call(
        functools.partial(_hgrn_kernel, reverse=reverse),
        grid=(batch, n_s),
        in_specs=[spec, spec, spec],
        out_specs=spec,
        out_shape=jax.ShapeDtypeStruct((T, REC_WIDTH), F32),
        scratch_shapes=[pltpu.VMEM((REC_HEADS, REC_DK, REC_DK), F32)],
        compiler_params=_params(2),
        name="hgrn_bwd" if reverse else "hgrn_fwd",
    )(qs, f, v)


def _attn_kernel(sink_ref, q_ref, kp_ref, kc_ref, kn_ref, vp_ref, vc_ref, vn_ref, o_ref, *, seq_len):
    blk = pl.program_id(1)
    n_q, n_k = ATT_BLOCK, 3 * ATT_BLOCK
    qi = lax.broadcasted_iota(jnp.int32, (n_q, n_k), 0)
    kj = lax.broadcasted_iota(jnp.int32, (n_q, n_k), 1)
    kpos = blk * ATT_BLOCK - ATT_BLOCK + kj
    valid = (jnp.abs(kj - ATT_BLOCK - qi) <= WINDOW) & (kpos >= 0) & (kpos < seq_len)
    q = q_ref[...]
    k_band = jnp.concatenate([kp_ref[...], kc_ref[...], kn_ref[...]], axis=0)
    v_band = jnp.concatenate([vp_ref[...], vc_ref[...], vn_ref[...]], axis=0)
    for g in range(ATT_KV_HEADS):
        ksl = slice(g * ATT_HEAD_DIM, (g + 1) * ATT_HEAD_DIM)
        k_g, v_g = k_band[:, ksl], v_band[:, ksl]
        for hh in range(ATT_GROUP):
            h = g * ATT_GROUP + hh
            hsl = slice(h * ATT_HEAD_DIM, (h + 1) * ATT_HEAD_DIM)
            s = _dot(q[:, hsl], k_g, ((1,), (1,)))
            s = jnp.where(valid, s, -1e30)
            sink = sink_ref[h]
            m = jnp.maximum(jnp.max(s, axis=-1, keepdims=True), sink)
            p = jnp.exp(s - m)
            den = jnp.sum(p, axis=-1, keepdims=True) + jnp.exp(sink - m)
            probs = (p / den).astype(BF16)
            o_ref[:, hsl] = _dot(probs, v_g).astype(o_ref.dtype)


def _attn(qa, ka, va, sink, batch, seq_len):
    T = qa.shape[0]
    n_b = seq_len // ATT_BLOCK
    cur = lambda b, i: (b * n_b + i, 0)
    prev = lambda b, i: (b * n_b + jnp.maximum(i - 1, 0), 0)
    nxt = lambda b, i: (b * n_b + jnp.minimum(i + 1, n_b - 1), 0)
    kv = lambda im: pl.BlockSpec((ATT_BLOCK, KV_WIDTH), im)
    return pl.pallas_call(
        functools.partial(_attn_kernel, seq_len=seq_len),
        grid=(batch, n_b),
        in_specs=[pl.BlockSpec(memory_space=pltpu.SMEM),
                  pl.BlockSpec((ATT_BLOCK, ATT_WIDTH), cur),
                  kv(prev), kv(cur), kv(nxt), kv(prev), kv(cur), kv(nxt)],
        out_specs=pl.BlockSpec((ATT_BLOCK, ATT_WIDTH), cur),
        out_shape=jax.ShapeDtypeStruct((T, ATT_WIDTH), BF16),
        compiler_params=_params(2),
        name="attn",
    )(sink, qa, ka, ka, ka, va, va, va)


def _mix_kernel(of_ref, ob_ref, gs_ref, att_ref, x_ref, g0_ref, b0_ref, ng_ref, wo_ref,
                g1_ref, b1_ref, wr_ref, x1_ref, x1b_ref, aff_ref):
    o_rec = of_ref[...] + ob_ref[...]
    gs = gs_ref[...]
    ng = ng_ref[...]
    recs = []
    for h in range(REC_HEADS):
        sl = slice(h * REC_DK, (h + 1) * REC_DK)
        oh = o_rec[:, sl]
        ms = jnp.mean(oh * oh, axis=-1, keepdims=True)
        recs.append(oh * lax.rsqrt(ms + NORM_EPS) * ng * gs[:, sl])
    rec = jnp.concatenate(recs, axis=1).astype(BF16)
    mix = _dot(rec, wo_ref[0:REC_WIDTH, :]) + _dot(att_ref[...], wo_ref[REC_WIDTH:, :])
    x0 = _layer_norm(x_ref[...], g0_ref[...], b0_ref[...])
    x1 = _layer_norm(DEEPNORM_ALPHA * x0 + mix, g1_ref[...], b1_ref[...])
    x1_ref[...] = x1
    x1b_ref[...] = x1.astype(BF16)
    logits = _dot(wr_ref[...], x1, ((1,), (1,)), precision=HIGHEST)
    m = jnp.max(logits, axis=0, keepdims=True)
    e = jnp.exp(logits - m)
    aff_ref[...] = e / jnp.sum(e, axis=0, keepdims=True)


def _mix(of, ob, gs, att, x2d, g0, b0, ng, w_out_bf16, g1, b1, w_router_t):
    T = x2d.shape[0]
    tm = TOKEN_TILE
    row = lambda i: (i, 0)
    wide = lambda w: pl.BlockSpec((tm, w), row)
    vec = _resident((1, D_MODEL))
    return pl.pallas_call(
        _mix_kernel,
        grid=(T // tm,),
        in_specs=[wide(REC_WIDTH), wide(REC_WIDTH), wide(REC_WIDTH), wide(ATT_WIDTH), wide(D_MODEL),
                  vec, vec, _resident((1, REC_DK)), _resident((D_MODEL, D_MODEL)), vec, vec,
                  _resident((N_EXPERTS, D_MODEL))],
        out_specs=[wide(D_MODEL), wide(D_MODEL), pl.BlockSpec((N_EXPERTS, tm), lambda i: (0, i))],
        out_shape=(jax.ShapeDtypeStruct((T, D_MODEL), F32),
                   jax.ShapeDtypeStruct((T, D_MODEL), BF16),
                   jax.ShapeDtypeStruct((N_EXPERTS, T), F32)),
        compiler_params=_params(1),
        name="mix",
    )(of, ob, gs, att, x2d, g0, b0, ng, w_out_bf16, g1, b1, w_router_t)


def _select_kernel(aff_ref, pos_ref, cs_ref, *, cap, slot_offset):
    aff = aff_ref[0]
    nb, w = aff.shape
    bits = pltpu.bitcast(aff, jnp.int32)
    capf = jnp.float32(cap)

    def total(mask):
        x = mask.astype(F32)
        return jnp.sum(jnp.sum(x, axis=0, keepdims=True), axis=1, keepdims=True)

    def search(i, prefix):
        cand = prefix | lax.shift_left(jnp.int32(1), 30 - i)
        return jnp.where(total(bits >= cand) >= capf, cand, prefix)

    thr = lax.fori_loop(0, 31, search, jnp.zeros((1, 1), jnp.int32))

    c_i = lax.broadcasted_iota(jnp.int32, (w, w), 0)
    c_j = lax.broadcasted_iota(jnp.int32, (w, w), 1)
    upper = (c_i <= c_j).astype(BF16)
    r_i = lax.broadcasted_iota(jnp.int32, (nb, nb), 0)
    r_j = lax.broadcasted_iota(jnp.int32, (nb, nb), 1)
    strict_lower = (r_j < r_i).astype(BF16)

    def prefix_counts(mask):
        x = mask.astype(F32)
        inc = _dot(x.astype(BF16), upper)
        tot = jnp.broadcast_to(inc[:, w - 1:w], (nb, 128)).astype(BF16)
        before = _dot(strict_lower, tot)
        return inc - x + before[:, 0:1], before

    gt = bits > thr
    eq = bits == thr
    need = capf - total(gt)
    eq_rank, _ = prefix_counts(eq)
    sel = gt | (eq & (eq_rank < need))
    rank, before = prefix_counts(sel)
    pos_ref[0] = jnp.where(sel, rank.astype(jnp.int32) + slot_offset, -1)
    cs_ref[0] = before.astype(jnp.int32) + slot_offset


def _select(aff_blocks, cap, slot_offset):
    n_e, nb, w = aff_blocks.shape
    spec = pl.BlockSpec((1, nb, w), lambda e: (e, 0, 0))
    return pl.pallas_call(
        functools.partial(_select_kernel, cap=cap, slot_offset=slot_offset),
        grid=(n_e,),
        in_specs=[spec],
        out_specs=[spec, pl.BlockSpec((1, nb, 128), lambda e: (e, 0, 0))],
        out_shape=(jax.ShapeDtypeStruct((n_e, nb, w), jnp.int32),
                   jax.ShapeDtypeStruct((n_e, nb, 128), jnp.int32)),
        compiler_params=_params(1),
        name="select",
    )(aff_blocks)


def _expert_kernel(blo_ref, nblk_ref, xa_ref, xb_ref, pos_ref, wg_ref, wu_ref, wd_ref, y_ref,
                   xbuf, sem, acc_ref, *, n_tiles, nb_first):
    e, j = pl.program_id(0), pl.program_id(1)
    R = DISPATCH_TILE
    b0 = blo_ref[e * n_tiles + j]
    nb = nblk_ref[e * n_tiles + j]

    def copy_from(src_ref, row_block, slot):
        return pltpu.make_async_copy(src_ref.at[pl.ds(row_block * R, R), :], xbuf.at[slot], sem.at[slot])

    def start(b, slot):
        @pl.when(b < nb_first)
        def _():
            copy_from(xa_ref, b, slot).start()

        @pl.when(b >= nb_first)
        def _():
            copy_from(xb_ref, b - nb_first, slot).start()

    start(b0, 0)
    acc_ref[...] = jnp.zeros_like(acc_ref)
    slot_ids = lax.broadcasted_iota(jnp.int32, (R, R), 0) + j * R

    def body(k, carry):
        slot = k % 2

        @pl.when(k + 1 < nb)
        def _():
            start(b0 + k + 1, 1 - slot)

        copy_from(xa_ref, 0, slot).wait()
        prow = pos_ref[0, pl.ds(b0 + k, 1), :]
        onehot = jnp.where(prow == slot_ids, 1.0, 0.0).astype(BF16)
        acc_ref[...] += _dot(onehot, xbuf[slot])
        return carry

    lax.fori_loop(0, nb, body, 0)
    xg = acc_ref[...].astype(BF16)
    y = jnp.zeros((R, D_MODEL), F32)
    for c in range(EXPERT_FF // FF_CHUNK):
        cs = slice(c * FF_CHUNK, (c + 1) * FF_CHUNK)
        h1 = _dot(xg, wg_ref[0, :, cs])
        h2 = _dot(xg, wu_ref[0, :, cs])
        h = (h1 * jax.nn.sigmoid(h1) * h2).astype(BF16)
        y = y + _dot(h, wd_ref[0, cs, :])
    y_ref[0] = y.astype(y_ref.dtype)


def _experts(blo, nblk, x1b_a, x1b_b, pos_all, wg, wu, wd, n_tiles):
    n_e, nb_tot, R = pos_all.shape
    nb_first = x1b_a.shape[0] // R
    grid_spec = pltpu.PrefetchScalarGridSpec(
        num_scalar_prefetch=2,
        grid=(n_e, n_tiles),
        in_specs=[pl.BlockSpec(memory_space=pl.ANY), pl.BlockSpec(memory_space=pl.ANY),
                  pl.BlockSpec((1, nb_tot, R), lambda e, j, *_: (e, 0, 0)),
                  pl.BlockSpec((1, D_MODEL, EXPERT_FF), lambda e, j, *_: (e, 0, 0)),
                  pl.BlockSpec((1, D_MODEL, EXPERT_FF), lambda e, j, *_: (e, 0, 0)),
                  pl.BlockSpec((1, EXPERT_FF, D_MODEL), lambda e, j, *_: (e, 0, 0))],
        out_specs=pl.BlockSpec((1, R, D_MODEL), lambda e, j, *_: (e, j, 0)),
        scratch_shapes=[pltpu.VMEM((2, R, D_MODEL), BF16), pltpu.SemaphoreType.DMA((2,)),
                        pltpu.VMEM((R, D_MODEL), F32)],
    )
    return pl.pallas_call(
        functools.partial(_expert_kernel, n_tiles=n_tiles, nb_first=nb_first),
        grid_spec=grid_spec,
        out_shape=jax.ShapeDtypeStruct((n_e, n_tiles * R, D_MODEL), BF16),
        compiler_params=_params(2),
        name="experts",
    )(blo, nblk, x1b_a, x1b_b, pos_all, wg, wu, wd)


def _combine_kernel(j0_ref, cnt_ref, x1_ref, pos_ref, aff_ref, y_hbm, g2_ref, b2_ref, o_ref,
                    ybuf, sem, acc_ref, *, n_blocks):
    i = pl.program_id(0)
    R = DISPATCH_TILE
    pairs = [(e, w) for e in range(N_EXPERTS) for w in range(2)]

    def copy(e, w):
        j = j0_ref[e * n_blocks + i] + w
        return pltpu.make_async_copy(y_hbm.at[e, pl.ds(j * R, R), :], ybuf.at[2 * e + w], sem.at[2 * e + w])

    for e, w in pairs:
        @pl.when(cnt_ref[e * n_blocks + i] > w)
        def _(e=e, w=w):
            copy(e, w).start()

    acc_ref[...] = DEEPNORM_ALPHA * x1_ref[...]
    slot_iota = lax.broadcasted_iota(jnp.int32, (R, R), 0)
    for e, w in pairs:
        @pl.when(cnt_ref[e * n_blocks + i] > w)
        def _(e=e, w=w):
            copy(e, w).wait()
            slot_ids = slot_iota + (j0_ref[e * n_blocks + i] + w) * R
            prow = pos_ref[0, e:e + 1, :]
            grow = aff_ref[0, e:e + 1, :]
            gated = jnp.where(prow == slot_ids, grow, 0.0).astype(BF16)
            acc_ref[...] += _dot(gated, ybuf[2 * e + w], ((0,), (0,)))

    o_ref[...] = _layer_norm(acc_ref[...], g2_ref[...], b2_ref[...])


def _combine(j0, cnt, x1, pos_blocks, aff_blocks, y_all, g2, b2, block_offset):
    T = x1.shape[0]
    R = DISPATCH_TILE
    n_blocks = T // R
    grid_spec = pltpu.PrefetchScalarGridSpec(
        num_scalar_prefetch=2,
        grid=(n_blocks,),
        in_specs=[pl.BlockSpec((R, D_MODEL), lambda i, *_: (i, 0)),
                  pl.BlockSpec((1, N_EXPERTS, R), lambda i, *_: (i + block_offset, 0, 0)),
                  pl.BlockSpec((1, N_EXPERTS, R), lambda i, *_: (i + block_offset, 0, 0)),
                  pl.BlockSpec(memory_space=pl.ANY),
                  pl.BlockSpec((1, D_MODEL), lambda i, *_: (0, 0)),
                  pl.BlockSpec((1, D_MODEL), lambda i, *_: (0, 0))],
        out_specs=pl.BlockSpec((R, D_MODEL), lambda i, *_: (i, 0)),
        scratch_shapes=[pltpu.VMEM((2 * N_EXPERTS, R, D_MODEL), BF16),
                        pltpu.SemaphoreType.DMA((2 * N_EXPERTS,)),
                        pltpu.VMEM((R, D_MODEL), F32)],
    )
    return pl.pallas_call(
        functools.partial(_combine_kernel, n_blocks=n_blocks),
        grid_spec=grid_spec,
        out_shape=jax.ShapeDtypeStruct((T, D_MODEL), F32),
        compiler_params=_params(1),
        name="combine",
    )(j0, cnt, x1, pos_blocks, aff_blocks, y_all, g2, b2)


def _rotary_tables(seq_len):
    half = ROT_DIM // 2
    inv = ROPE_THETA ** (-jnp.arange(half, dtype=F32) / half)
    ang = jnp.arange(seq_len, dtype=F32)[:, None] * inv[None, :]
    cos, sin = jnp.cos(ang), jnp.sin(ang)
    pad = ATT_HEAD_DIM - ROT_DIM
    ones = jnp.ones((seq_len, pad), F32)
    zeros = jnp.zeros((seq_len, pad + half), F32)
    rc = jnp.concatenate([cos, cos, ones], axis=1)
    rsa = jnp.concatenate([-sin, zeros], axis=1)
    rsb = jnp.concatenate([jnp.zeros((seq_len, half), F32), sin, zeros[:, :pad]], axis=1)
    two = lambda t: jnp.concatenate([t, t], axis=1)
    return two(rc), two(rsa), two(rsb)


def _dispatch_plan(cs_all, cap_tot):
    R = DISPATCH_TILE
    n_e, nb_tot = cs_all.shape
    cs_ext = jnp.concatenate([cs_all, jnp.full((n_e, 1), cap_tot, jnp.int32)], axis=1)
    lo = jnp.arange(cap_tot // R, dtype=jnp.int32) * R
    blo = jax.vmap(lambda c: jnp.searchsorted(c[1:], lo, side="right"))(cs_ext).astype(jnp.int32)
    bhi = jax.vmap(lambda c: jnp.searchsorted(c, lo + R, side="left"))(cs_all).astype(jnp.int32) - 1
    nblk = bhi - blo + 1
    c0, c1 = cs_ext[:, :-1], cs_ext[:, 1:]
    j0 = c0 // R
    cnt = jnp.where(c1 > c0, (c1 - 1) // R - j0 + 1, 0)
    return blo.reshape(-1), nblk.reshape(-1), j0, cnt


def _trunk_front(x, seq_tables, p):
    B, S, D = x.shape
    T = B * S
    x2d = x.reshape(T, D)
    rc, rsa, rsb = seq_tables
    qs, ff, fb, v, gs, qa, ka, va = _in_proj(x2d, p["g0"], p["b0"], p["w_in"], p["lbf"], p["lbb"],
                                             rc, rsa, rsb, S)
    of = _hgrn(qs, ff, v, B, S, reverse=False)
    ob = _hgrn(qs, fb, v, B, S, reverse=True)
    att = _attn(qa, ka, va, p["sink"], B, S)
    return _mix(of, ob, gs, att, x2d, p["g0"], p["b0"], p["ng"], p["w_out"], p["g1"], p["b1"], p["w_router_t"])


def kernel(x_prompt, x_sample, emb_ln_g, emb_ln_b, w_in, hgrn_lb_fwd, hgrn_lb_bwd, hgrn_norm_g, attn_sink, w_out, ln1_g, ln1_b, w_router, w_gate, w_up, w_down, ln2_g, ln2_b):
    assert w_in.shape[0] == 1, "single-layer problem"
    R = DISPATCH_TILE
    p = dict(
        g0=emb_ln_g.reshape(1, D_MODEL), b0=emb_ln_b.reshape(1, D_MODEL),
        w_in=w_in[0].astype(BF16), lbf=hgrn_lb_fwd, lbb=hgrn_lb_bwd,
        ng=hgrn_norm_g[0].reshape(1, REC_DK), sink=attn_sink[0],
        w_out=w_out[0].astype(BF16), g1=ln1_g[0].reshape(1, D_MODEL), b1=ln1_b[0].reshape(1, D_MODEL),
        w_router_t=w_router[0].T,
    )
    g2, b2 = ln2_g[0].reshape(1, D_MODEL), ln2_b[0].reshape(1, D_MODEL)
    wg, wu, wd = w_gate[0].astype(BF16), w_up[0].astype(BF16), w_down[0].astype(BF16)

    groups = []
    slot_offset = 0
    for x in (x_prompt, x_sample):
        B, S, _ = x.shape
        T = B * S
        cap = CAPACITY_FACTOR * T // N_EXPERTS
        assert T % R == 0 and cap % R == 0
        x1, x1b, aff = _trunk_front(x, _rotary_tables(S), p)
        aff_blocks = aff.reshape(N_EXPERTS, T // R, R)
        pos, cs = _select(aff_blocks, cap, slot_offset)
        groups.append(dict(x=x, x1=x1, x1b=x1b, aff=aff_blocks, pos=pos, cs=cs[:, :, 0]))
        slot_offset += cap
    cap_tot = slot_offset

    pos_all = jnp.concatenate([g["pos"] for g in groups], axis=1)
    aff_all = jnp.concatenate([g["aff"] for g in groups], axis=1)
    cs_all = jnp.concatenate([g["cs"] for g in groups], axis=1)
    blo, nblk, j0, cnt = _dispatch_plan(cs_all, cap_tot)
    y_all = _experts(blo, nblk, groups[0]["x1b"], groups[1]["x1b"], pos_all, wg, wu, wd, cap_tot // R)

    pos_blocks = jnp.transpose(pos_all, (1, 0, 2))
    aff_blocks = jnp.transpose(aff_all, (1, 0, 2))
    outs = []
    block_offset = 0
    for g in groups:
        nb = g["x1"].shape[0] // R
        sl = slice(block_offset, block_offset + nb)
        outs.append(_combine(j0[:, sl].reshape(-1), cnt[:, sl].reshape(-1), g["x1"], pos_blocks, aff_blocks,
                             y_all, g2, b2, block_offset).reshape(g["x"].shape))
        block_offset += nb
    return tuple(outs)
```

```python
import functools

import jax
import jax.numpy as jnp
import numpy as np
from jax import lax
from jax.experimental import pallas as pl
from jax.experimental.pallas import tpu as pltpu

D_MODEL = 1024
REC_WIDTH = 512
ATT_WIDTH = 512
REC_HEADS = 4
REC_DK = 128
REC_CHUNK = 64
ATT_HEADS = 8
ATT_KV_HEADS = 2
ATT_HEAD_DIM = 64
ATT_GROUP = 4
WINDOW = 128
ATT_BLOCK = 128
ROT_DIM = 16
ROPE_THETA = 500000.0
N_EXPERTS = 16
EXPERT_FF = 2048
CAPACITY_FACTOR = 2
NORM_EPS = 1e-5
DEEPNORM_ALPHA = 2.0 ** 0.25
KV_WIDTH = ATT_KV_HEADS * ATT_HEAD_DIM
IN_PROJ_WIDTH = 5 * REC_WIDTH + ATT_WIDTH + 2 * KV_WIDTH

TOKEN_TILE = 512
SEQ_TILE = 512
DISPATCH_TILE = 256
GATHER_BLOCK = 512
GATHER_RING = 4
FF_CHUNK = 512
VMEM_LIMIT_BYTES = 56 * 1024 * 1024

F32 = jnp.float32
BF16 = jnp.bfloat16
HIGHEST = lax.Precision.HIGHEST


def _params(n_axes):
    return pltpu.CompilerParams(dimension_semantics=("arbitrary",) * n_axes,
                                vmem_limit_bytes=VMEM_LIMIT_BYTES)


def _resident(shape):
    nd = len(shape)
    return pl.BlockSpec(shape, lambda *_: (0,) * nd, pipeline_mode=pl.Buffered(1))


def _layer_norm(x, g, b):
    mu = jnp.mean(x, axis=-1, keepdims=True)
    xc = x - mu
    var = jnp.mean(xc * xc, axis=-1, keepdims=True)
    return xc * lax.rsqrt(var + NORM_EPS) * g + b


def _lower_bound(p):
    p0, p1 = p[0:1, :], p[1:2, :]
    m = jnp.maximum(p0, p1)
    e0, e1 = jnp.exp(p0 - m), jnp.exp(p1 - m)
    return e0 / (e0 + e1)


def _dot(a, b, dims=((1,), (0,)), precision=None):
    return lax.dot_general(a, b, (dims, ((), ())), preferred_element_type=F32, precision=precision)


def _in_proj_kernel(x_ref, g_ref, b_ref, w_ref, lbf_ref, lbb_ref, rc_ref, rsa_ref, rsb_ref,
                    qs_ref, ff_ref, fb_ref, v_ref, gs_ref, qa_ref, kv_ref):
    xn = _layer_norm(x_ref[...], g_ref[...], b_ref[...])
    proj = _dot(xn.astype(BF16), w_ref[...])
    W = REC_WIDTH
    q_r, zf, zb = proj[:, 0:W], proj[:, W:2 * W], proj[:, 2 * W:3 * W]
    i_r, g_r = proj[:, 3 * W:4 * W], proj[:, 4 * W:5 * W]
    q_a = proj[:, 5 * W:5 * W + ATT_WIDTH]
    k_a = proj[:, 5 * W + ATT_WIDTH:5 * W + ATT_WIDTH + KV_WIDTH]
    v_a = proj[:, 5 * W + ATT_WIDTH + KV_WIDTH:]
    lbf = _lower_bound(lbf_ref[...])
    lbb = _lower_bound(lbb_ref[...])
    qs_ref[...] = q_r * jax.nn.sigmoid(q_r)
    ff_ref[...] = lbf + (1.0 - lbf) * jax.nn.sigmoid(zf)
    fb_ref[...] = lbb + (1.0 - lbb) * jax.nn.sigmoid(zb)
    v_ref[...] = i_r.astype(BF16)
    gs_ref[...] = g_r * jax.nn.sigmoid(g_r)
    rc, rsa, rsb = rc_ref[...], rsa_ref[...], rsb_ref[...]
    k_rot = k_a * rc + pltpu.roll(k_a, KV_WIDTH - ROT_DIM // 2, 1) * rsa + pltpu.roll(k_a, ROT_DIM // 2, 1) * rsb
    half_kv = KV_WIDTH // 2
    kv_ref[:, 0:KV_WIDTH] = k_rot.astype(BF16)
    kv_ref[:, KV_WIDTH:2 * KV_WIDTH] = pltpu.roll(k_rot, half_kv, 1).astype(BF16)
    kv_ref[:, 2 * KV_WIDTH:3 * KV_WIDTH] = v_a.astype(BF16)
    kv_ref[:, 3 * KV_WIDTH:] = pltpu.roll(v_a, half_kv, 1).astype(BF16)
    n_rep = ATT_WIDTH // KV_WIDTH
    rc4 = jnp.concatenate([rc] * n_rep, axis=1)
    rsa4 = jnp.concatenate([rsa] * n_rep, axis=1)
    rsb4 = jnp.concatenate([rsb] * n_rep, axis=1)
    q_rot = q_a * rc4 + pltpu.roll(q_a, ATT_WIDTH - ROT_DIM // 2, 1) * rsa4 + pltpu.roll(q_a, ROT_DIM // 2, 1) * rsb4
    qa_ref[...] = (q_rot * (ATT_HEAD_DIM ** -0.5)).astype(BF16)


def _in_proj(x2d, g, b, w_in_bf16, lbf, lbb, rc, rsa, rsb, seq_len):
    T = x2d.shape[0]
    tm = TOKEN_TILE
    assert T % tm == 0 and seq_len % tm == 0
    n_seq = seq_len // tm
    row = lambda i: (i, 0)
    pos_row = lambda i: (i % n_seq, 0)
    wide = lambda w: pl.BlockSpec((tm, w), row)
    out_shapes = (
        jax.ShapeDtypeStruct((T, REC_WIDTH), F32),
        jax.ShapeDtypeStruct((T, REC_WIDTH), F32),
        jax.ShapeDtypeStruct((T, REC_WIDTH), F32),
        jax.ShapeDtypeStruct((T, REC_WIDTH), BF16),
        jax.ShapeDtypeStruct((T, REC_WIDTH), F32),
        jax.ShapeDtypeStruct((T, ATT_WIDTH), BF16),
        jax.ShapeDtypeStruct((T, 4 * KV_WIDTH), BF16),
    )
    return pl.pallas_call(
        _in_proj_kernel,
        grid=(T // tm,),
        in_specs=[wide(D_MODEL), _resident((1, D_MODEL)), _resident((1, D_MODEL)),
                  _resident((D_MODEL, IN_PROJ_WIDTH)), _resident((2, REC_WIDTH)), _resident((2, REC_WIDTH)),
                  pl.BlockSpec((tm, KV_WIDTH), pos_row), pl.BlockSpec((tm, KV_WIDTH), pos_row),
                  pl.BlockSpec((tm, KV_WIDTH), pos_row)],
        out_specs=[wide(REC_WIDTH)] * 5 + [wide(ATT_WIDTH), wide(4 * KV_WIDTH)],
        out_shape=out_shapes,
        compiler_params=_params(1),
        name="in_proj",
    )(x2d, g, b, w_in_bf16, lbf, lbb, rc, rsa, rsb)


def _hgrn_kernel(q_ref, f_ref, v_ref, o_ref, state_ref, *, reverse):
    C = REC_CHUNK
    n_chunks = q_ref.shape[0] // C

    @pl.when(pl.program_id(1) == 0)
    def _():
        state_ref[...] = jnp.zeros_like(state_ref)

    t_i = lax.broadcasted_iota(jnp.int32, (C, C), 0)
    s_i = lax.broadcasted_iota(jnp.int32, (C, C), 1)
    causal = (s_i >= t_i) if reverse else (s_i <= t_i)
    csum = causal.astype(F32)
    mid = C // 2 if reverse else C // 2 - 1
    last = 0 if reverse else C - 1

    def chunk(ci, carry):
        c = (n_chunks - 1 - ci) if reverse else ci
        r0 = pl.multiple_of(c * C, C)
        f = f_ref[pl.ds(r0, C), :]
        q = q_ref[pl.ds(r0, C), :]
        v = v_ref[pl.ds(r0, C), :]
        kk = 1.0 - f
        b = _dot(csum, jnp.log(f), precision=HIGHEST)
        b_mid = b[mid:mid + 1, :]
        b_last = b[last:last + 1, :]
        qe = (q * jnp.exp(b - b_mid)).astype(BF16)
        ke = (kk * jnp.exp(b_mid - b)).astype(BF16)
        ku = (kk * jnp.exp(b_last - b)).astype(BF16)
        qb = (q * jnp.exp(b)).astype(BF16)
        decay = jnp.exp(b_last)
        outs = []
        for h in range(REC_HEADS):
            sl = slice(h * REC_DK, (h + 1) * REC_DK)
            a = _dot(qe[:, sl], ke[:, sl], ((1,), (1,)))
            a = jnp.where(causal, a, 0.0).astype(BF16)
            st = state_ref[h]
            o = _dot(a, v[:, sl]) + _dot(qb[:, sl], st.astype(BF16), ((1,), (1,)))
            u_t = _dot(v[:, sl], ku[:, sl], ((0,), (0,)))
            state_ref[h] = st * decay[:, sl] + u_t
            outs.append(o)
        o_ref[pl.ds(r0, C), :] = jnp.concatenate(outs, axis=1)
        return carry

    lax.fori_loop(0, n_chunks, chunk, 0, unroll=True)


def _hgrn(qs, f, v, batch, seq_len, reverse):
    T = qs.shape[0]
    tc = SEQ_TILE
    assert seq_len % tc == 0
    n_s = seq_len // tc
    if reverse:
        idx = lambda b, s: (b * n_s + (n_s - 1 - s), 0)
    else:
        idx = lambda b, s: (b * n_s + s, 0)
    spec = pl.BlockSpec((tc, REC_WIDTH), idx)
    return pl.pallas_call(
        functools.partial(_hgrn_kernel, reverse=reverse),
        grid=(batch, n_s),
        in_specs=[spec, spec, spec],
        out_specs=spec,
        out_shape=jax.ShapeDtypeStruct((T, REC_WIDTH), F32),
        scratch_shapes=[pltpu.VMEM((REC_HEADS, REC_DK, REC_DK), F32)],
        compiler_params=_params(2),
        name="hgrn_bwd" if reverse else "hgrn_fwd",
    )(qs, f, v)


ATT_ROW_CHUNK = 32


def _attn_kernel(sink_ref, q_ref, kvp_ref, kvc_ref, kvn_ref, o_ref, s_ref, p_ref, bias_ref, *, seq_len):
    blk = pl.program_id(1)
    n_q, n_k = ATT_BLOCK, 3 * ATT_BLOCK
    qi = lax.broadcasted_iota(jnp.int32, (n_q, n_k), 0)
    kj = lax.broadcasted_iota(jnp.int32, (n_q, n_k), 1)
    kpos = blk * ATT_BLOCK - ATT_BLOCK + kj
    valid = (jnp.abs(kj - ATT_BLOCK - qi) <= WINDOW) & (kpos >= 0) & (kpos < seq_len)
    bias_ref[...] = jnp.where(valid, 0.0, -1e30)
    band = jnp.concatenate([kvp_ref[...], kvc_ref[...], kvn_ref[...]], axis=0)
    low = lax.broadcasted_iota(jnp.int32, (n_k, KV_WIDTH), 1) < ATT_HEAD_DIM
    zero = jnp.zeros((n_k, KV_WIDTH), BF16)
    k_plain, k_swap = band[:, 0:KV_WIDTH], band[:, KV_WIDTH:2 * KV_WIDTH]
    v_plain, v_swap = band[:, 2 * KV_WIDTH:3 * KV_WIDTH], band[:, 3 * KV_WIDTH:]
    slab = 2 * ATT_HEAD_DIM
    for g in range(ATT_KV_HEADS):
        src_lo, src_hi = (k_plain, k_swap) if g == 0 else (k_swap, k_plain)
        vsrc_lo, vsrc_hi = (v_plain, v_swap) if g == 0 else (v_swap, v_plain)
        k_lo, k_hi = jnp.where(low, src_lo, zero), jnp.where(low, zero, src_hi)
        v_lo, v_hi = jnp.where(low, vsrc_lo, zero), jnp.where(low, zero, vsrc_hi)
        q2 = jnp.concatenate([q_ref[:, (2 * g) * slab:(2 * g + 1) * slab],
                              q_ref[:, (2 * g + 1) * slab:(2 * g + 2) * slab]], axis=0)
        s_ref[0:2 * n_q, :] = _dot(q2, k_lo, ((1,), (1,)))
        s_ref[2 * n_q:4 * n_q, :] = _dot(q2, k_hi, ((1,), (1,)))
        heads = (4 * g, 4 * g + 2, 4 * g + 1, 4 * g + 3)
        for r in range(4 * n_q // ATT_ROW_CHUNK):
            r0 = r * ATT_ROW_CHUNK
            q0 = r0 % n_q
            sink = sink_ref[heads[r0 // n_q]]
            s = s_ref[r0:r0 + ATT_ROW_CHUNK, :] + bias_ref[q0:q0 + ATT_ROW_CHUNK, :]
            m = jnp.maximum(jnp.max(s, axis=-1, keepdims=True), sink)
            p = jnp.exp(s - m)
            den = jnp.sum(p, axis=-1, keepdims=True) + jnp.exp(sink - m)
            p_ref[r0:r0 + ATT_ROW_CHUNK, :] = (p / den).astype(BF16)
        for i in range(2):
            lo_rows = p_ref[i * n_q:(i + 1) * n_q, :]
            hi_rows = p_ref[(2 + i) * n_q:(3 + i) * n_q, :]
            o = _dot(lo_rows, v_lo) + _dot(hi_rows, v_hi)
            o_ref[:, (2 * g + i) * slab:(2 * g + i + 1) * slab] = o.astype(o_ref.dtype)


def _attn(qa, kv, sink, batch, seq_len):
    T = qa.shape[0]
    n_b = seq_len // ATT_BLOCK
    cur = lambda b, i: (b * n_b + i, 0)
    prev = lambda b, i: (b * n_b + jnp.maximum(i - 1, 0), 0)
    nxt = lambda b, i: (b * n_b + jnp.minimum(i + 1, n_b - 1), 0)
    kvs = lambda im: pl.BlockSpec((ATT_BLOCK, 4 * KV_WIDTH), im)
    n_rows = ATT_GROUP * ATT_BLOCK
    return pl.pallas_call(
        functools.partial(_attn_kernel, seq_len=seq_len),
        grid=(batch, n_b),
        in_specs=[pl.BlockSpec(memory_space=pltpu.SMEM),
                  pl.BlockSpec((ATT_BLOCK, ATT_WIDTH), cur), kvs(prev), kvs(cur), kvs(nxt)],
        out_specs=pl.BlockSpec((ATT_BLOCK, ATT_WIDTH), cur),
        out_shape=jax.ShapeDtypeStruct((T, ATT_WIDTH), BF16),
        scratch_shapes=[pltpu.VMEM((n_rows, 3 * ATT_BLOCK), F32), pltpu.VMEM((n_rows, 3 * ATT_BLOCK), BF16),
                        pltpu.VMEM((ATT_BLOCK, 3 * ATT_BLOCK), F32)],
        compiler_params=_params(2),
        name="attn",
    )(sink, qa, kv, kv, kv)


def _mix_kernel(of_ref, ob_ref, gs_ref, att_ref, x_ref, g0_ref, b0_ref, ng_ref, wo_ref,
                g1_ref, b1_ref, wr_ref, x1_ref, x1b_ref, aff_ref):
    o_rec = of_ref[...] + ob_ref[...]
    gs = gs_ref[...]
    ng = ng_ref[...]
    recs = []
    for h in range(REC_HEADS):
        sl = slice(h * REC_DK, (h + 1) * REC_DK)
        oh = o_rec[:, sl]
        ms = jnp.mean(oh * oh, axis=-1, keepdims=True)
        recs.append(oh * lax.rsqrt(ms + NORM_EPS) * ng * gs[:, sl])
    rec = jnp.concatenate(recs, axis=1).astype(BF16)
    mix = _dot(rec, wo_ref[0:REC_WIDTH, :]) + _dot(att_ref[...], wo_ref[REC_WIDTH:, :])
    x0 = _layer_norm(x_ref[...], g0_ref[...], b0_ref[...])
    x1 = _layer_norm(DEEPNORM_ALPHA * x0 + mix, g1_ref[...], b1_ref[...])
    x1_ref[...] = x1
    x1b_ref[...] = x1.astype(BF16)
    logits = _dot(wr_ref[...], x1, ((1,), (1,)), precision=HIGHEST)
    m = jnp.max(logits, axis=0, keepdims=True)
    e = jnp.exp(logits - m)
    aff_ref[...] = e / jnp.sum(e, axis=0, keepdims=True)


def _mix(of, ob, gs, att, x2d, g0, b0, ng, w_out_bf16, g1, b1, w_router_t):
    T = x2d.shape[0]
    tm = TOKEN_TILE
    row = lambda i: (i, 0)
    wide = lambda w: pl.BlockSpec((tm, w), row)
    vec = _resident((1, D_MODEL))
    return pl.pallas_call(
        _mix_kernel,
        grid=(T // tm,),
        in_specs=[wide(REC_WIDTH), wide(REC_WIDTH), wide(REC_WIDTH), wide(ATT_WIDTH), wide(D_MODEL),
                  vec, vec, _resident((1, REC_DK)), _resident((D_MODEL, D_MODEL)), vec, vec,
                  _resident((N_EXPERTS, D_MODEL))],
        out_specs=[wide(D_MODEL), wide(D_MODEL), pl.BlockSpec((N_EXPERTS, tm), lambda i: (0, i))],
        out_shape=(jax.ShapeDtypeStruct((T, D_MODEL), F32),
                   jax.ShapeDtypeStruct((T, D_MODEL), BF16),
                   jax.ShapeDtypeStruct((N_EXPERTS, T), F32)),
        compiler_params=_params(1),
        name="mix",
    )(of, ob, gs, att, x2d, g0, b0, ng, w_out_bf16, g1, b1, w_router_t)


def _select_kernel(aff_ref, pos_ref, cs_ref, *, cap, slot_offset):
    aff = aff_ref[0]
    nb, w = aff.shape
    bits = pltpu.bitcast(aff, jnp.int32)
    capf = jnp.float32(cap)

    def total(mask):
        x = mask.astype(F32)
        return jnp.sum(jnp.sum(x, axis=0, keepdims=True), axis=1, keepdims=True)

    def search(i, prefix):
        cand = prefix | lax.shift_left(jnp.int32(1), 30 - i)
        return jnp.where(total(bits >= cand) >= capf, cand, prefix)

    thr = lax.fori_loop(0, 31, search, jnp.zeros((1, 1), jnp.int32))

    c_i = lax.broadcasted_iota(jnp.int32, (w, w), 0)
    c_j = lax.broadcasted_iota(jnp.int32, (w, w), 1)
    upper = (c_i <= c_j).astype(BF16)
    r_i = lax.broadcasted_iota(jnp.int32, (nb, nb), 0)
    r_j = lax.broadcasted_iota(jnp.int32, (nb, nb), 1)
    strict_lower = (r_j < r_i).astype(BF16)

    def prefix_counts(mask):
        x = mask.astype(F32)
        inc = _dot(x.astype(BF16), upper)
        tot = jnp.broadcast_to(inc[:, w - 1:w], (nb, 128)).astype(BF16)
        before = _dot(strict_lower, tot)
        return inc - x + before[:, 0:1], before

    gt = bits > thr
    eq = bits == thr
    need = capf - total(gt)
    eq_rank, _ = prefix_counts(eq)
    sel = gt | (eq & (eq_rank < need))
    rank, before = prefix_counts(sel)
    pos_ref[0] = jnp.where(sel, rank.astype(jnp.int32) + slot_offset, -1)
    cs_ref[0] = before.astype(jnp.int32) + slot_offset


def _select(aff_blocks, cap, slot_offset):
    n_e, nb, w = aff_blocks.shape
    spec = pl.BlockSpec((1, nb, w), lambda e: (e, 0, 0))
    return pl.pallas_call(
        functools.partial(_select_kernel, cap=cap, slot_offset=slot_offset),
        grid=(n_e,),
        in_specs=[spec],
        out_specs=[spec, pl.BlockSpec((1, nb, 128), lambda e: (e, 0, 0))],
        out_shape=(jax.ShapeDtypeStruct((n_e, nb, w), jnp.int32),
                   jax.ShapeDtypeStruct((n_e, nb, 128), jnp.int32)),
        compiler_params=_params(1),
        name="select",
    )(aff_blocks)


def _expert_kernel(blo_ref, nblk_ref, xa_ref, xb_ref, pos_ref, wg_ref, wu_ref, wd_ref, y_ref,
                   xbuf, sem, acc_ref, *, n_tiles, nb_first):
    e, j = pl.program_id(0), pl.program_id(1)
    R, G = DISPATCH_TILE, GATHER_BLOCK
    t = e * n_tiles + j
    n_steps = pl.num_programs(0) * n_tiles
    b0 = blo_ref[t]
    nb = nblk_ref[t]

    def copy_from(src_ref, row_block, slot):
        return pltpu.make_async_copy(src_ref.at[pl.ds(row_block * G, G), :], xbuf.at[slot], sem.at[slot])

    def start(b, slot):
        @pl.when(b < nb_first)
        def _():
            copy_from(xa_ref, b, slot).start()

        @pl.when(b >= nb_first)
        def _():
            copy_from(xb_ref, b - nb_first, slot).start()

    def start_ring(first_block, n):
        for r in range(GATHER_RING):
            @pl.when(r < n)
            def _(r=r):
                start(first_block + r, r)

    @pl.when(t == 0)
    def _():
        start_ring(b0, nb)

    acc_ref[...] = jnp.zeros_like(acc_ref)
    slot_ids = lax.broadcasted_iota(jnp.int32, (R, G), 0) + j * R

    def body(k, carry):
        slot = k % GATHER_RING
        copy_from(xa_ref, 0, slot).wait()
        prow = pos_ref[0, pl.ds(b0 + k, 1), :]
        onehot = jnp.where(prow == slot_ids, 1.0, 0.0).astype(BF16)
        acc_ref[...] += _dot(onehot, xbuf[slot])

        @pl.when(k + GATHER_RING < nb)
        def _():
            start(b0 + k + GATHER_RING, slot)

        return carry

    lax.fori_loop(0, nb, body, 0)

    @pl.when(t + 1 < n_steps)
    def _():
        start_ring(blo_ref[t + 1], nblk_ref[t + 1])

    xg = acc_ref[...].astype(BF16)
    y = jnp.zeros((R, D_MODEL), F32)
    for c in range(EXPERT_FF // FF_CHUNK):
        cs = slice(c * FF_CHUNK, (c + 1) * FF_CHUNK)
        h1 = _dot(xg, wg_ref[0, :, cs])
        h2 = _dot(xg, wu_ref[0, :, cs])
        h = (h1 * jax.nn.sigmoid(h1) * h2).astype(BF16)
        y = y + _dot(h, wd_ref[0, cs, :])
    y_ref[0] = y.astype(y_ref.dtype)


def _experts(blo, nblk, x1b_a, x1b_b, pos_gather, wg, wu, wd, n_tiles):
    n_e, nb_tot, G = pos_gather.shape
    R = DISPATCH_TILE
    assert G == GATHER_BLOCK and x1b_a.shape[0] % G == 0
    nb_first = x1b_a.shape[0] // G
    grid_spec = pltpu.PrefetchScalarGridSpec(
        num_scalar_prefetch=2,
        grid=(n_e, n_tiles),
        in_specs=[pl.BlockSpec(memory_space=pl.ANY), pl.BlockSpec(memory_space=pl.ANY),
                  pl.BlockSpec((1, nb_tot, G), lambda e, j, *_: (e, 0, 0)),
                  pl.BlockSpec((1, D_MODEL, EXPERT_FF), lambda e, j, *_: (e, 0, 0)),
                  pl.BlockSpec((1, D_MODEL, EXPERT_FF), lambda e, j, *_: (e, 0, 0)),
                  pl.BlockSpec((1, EXPERT_FF, D_MODEL), lambda e, j, *_: (e, 0, 0))],
        out_specs=pl.BlockSpec((1, R, D_MODEL), lambda e, j, *_: (e, j, 0)),
        scratch_shapes=[pltpu.VMEM((GATHER_RING, G, D_MODEL), BF16), pltpu.SemaphoreType.DMA((GATHER_RING,)),
                        pltpu.VMEM((R, D_MODEL), F32)],
    )
    return pl.pallas_call(
        functools.partial(_expert_kernel, n_tiles=n_tiles, nb_first=nb_first),
        grid_spec=grid_spec,
        out_shape=jax.ShapeDtypeStruct((n_e, n_tiles * R, D_MODEL), BF16),
        compiler_params=_params(2),
        name="experts",
    )(blo, nblk, x1b_a, x1b_b, pos_gather, wg, wu, wd)


def _combine_kernel(j0_ref, cnt_ref, x1_ref, pos_ref, aff_ref, y_hbm, g2_ref, b2_ref, o_ref,
                    ymain, yextra, sem_main, sem_extra, gate_ref, acc_ref, *, n_blocks):
    i = pl.program_id(0)
    R = DISPATCH_TILE
    half = i % 2

    def main_copy(e, blk, h):
        j = j0_ref[e * n_blocks + blk]
        return pltpu.make_async_copy(y_hbm.at[e, pl.ds(j * R, R), :], ymain.at[h, pl.ds(e * R, R), :],
                                     sem_main.at[h, e])

    def extra_copy(e):
        j = j0_ref[e * n_blocks + i] + 1
        return pltpu.make_async_copy(y_hbm.at[e, pl.ds(j * R, R), :], yextra.at[e], sem_extra.at[e])

    def start_main(blk, h):
        for e in range(N_EXPERTS):
            @pl.when(cnt_ref[e * n_blocks + blk] > 0)
            def _(e=e):
                main_copy(e, blk, h).start()

    @pl.when(i == 0)
    def _():
        ymain[...] = jnp.zeros_like(ymain)
        start_main(0, 0)

    @pl.when(i + 1 < n_blocks)
    def _():
        start_main(i + 1, 1 - half)

    for e in range(N_EXPERTS):
        @pl.when(cnt_ref[e * n_blocks + i] > 1)
        def _(e=e):
            extra_copy(e).start()

    slot_iota = lax.broadcasted_iota(jnp.int32, (R, R), 0)

    def gate_block(e, w):
        slot_ids = slot_iota + (j0_ref[e * n_blocks + i] + w) * R
        return jnp.where(pos_ref[0, e:e + 1, :] == slot_ids, aff_ref[0, e:e + 1, :], 0.0).astype(BF16)

    for e in range(N_EXPERTS):
        gate_ref[e * R:(e + 1) * R, :] = gate_block(e, 0)

    for e in range(N_EXPERTS):
        @pl.when(cnt_ref[e * n_blocks + i] > 0)
        def _(e=e):
            main_copy(e, i, half).wait()

    acc_ref[...] = DEEPNORM_ALPHA * x1_ref[...] + _dot(gate_ref[...], ymain[half], ((0,), (0,)))

    for e in range(N_EXPERTS):
        @pl.when(cnt_ref[e * n_blocks + i] > 1)
        def _(e=e):
            extra_copy(e).wait()
            acc_ref[...] += _dot(gate_block(e, 1), yextra[e], ((0,), (0,)))

    o_ref[...] = _layer_norm(acc_ref[...], g2_ref[...], b2_ref[...])


def _combine(j0, cnt, x1, pos_blocks, aff_blocks, y_all, g2, b2, block_offset):
    T = x1.shape[0]
    R = DISPATCH_TILE
    n_blocks = T // R
    grid_spec = pltpu.PrefetchScalarGridSpec(
        num_scalar_prefetch=2,
        grid=(n_blocks,),
        in_specs=[pl.BlockSpec((R, D_MODEL), lambda i, *_: (i, 0)),
                  pl.BlockSpec((1, N_EXPERTS, R), lambda i, *_: (i + block_offset, 0, 0)),
                  pl.BlockSpec((1, N_EXPERTS, R), lambda i, *_: (i + block_offset, 0, 0)),
                  pl.BlockSpec(memory_space=pl.ANY),
                  pl.BlockSpec((1, D_MODEL), lambda i, *_: (0, 0)),
                  pl.BlockSpec((1, D_MODEL), lambda i, *_: (0, 0))],
        out_specs=pl.BlockSpec((R, D_MODEL), lambda i, *_: (i, 0)),
        scratch_shapes=[pltpu.VMEM((2, N_EXPERTS * R, D_MODEL), BF16),
                        pltpu.VMEM((N_EXPERTS, R, D_MODEL), BF16),
                        pltpu.SemaphoreType.DMA((2, N_EXPERTS)),
                        pltpu.SemaphoreType.DMA((N_EXPERTS,)),
                        pltpu.VMEM((N_EXPERTS * R, R), BF16),
                        pltpu.VMEM((R, D_MODEL), F32)],
    )
    return pl.pallas_call(
        functools.partial(_combine_kernel, n_blocks=n_blocks),
        grid_spec=grid_spec,
        out_shape=jax.ShapeDtypeStruct((T, D_MODEL), F32),
        compiler_params=_params(1),
        name="combine",
    )(j0, cnt, x1, pos_blocks, aff_blocks, y_all, g2, b2)


def _rotary_tables(seq_len):
    half = ROT_DIM // 2
    inv = ROPE_THETA ** (-jnp.arange(half, dtype=F32) / half)
    ang = jnp.arange(seq_len, dtype=F32)[:, None] * inv[None, :]
    cos, sin = jnp.cos(ang), jnp.sin(ang)
    pad = ATT_HEAD_DIM - ROT_DIM
    ones = jnp.ones((seq_len, pad), F32)
    zeros = jnp.zeros((seq_len, pad + half), F32)
    rc = jnp.concatenate([cos, cos, ones], axis=1)
    rsa = jnp.concatenate([-sin, zeros], axis=1)
    rsb = jnp.concatenate([jnp.zeros((seq_len, half), F32), sin, zeros[:, :pad]], axis=1)
    two = lambda t: jnp.concatenate([t, t], axis=1)
    return two(rc), two(rsa), two(rsb)


def _dispatch_plan(cs_all, cap_tot):
    R = DISPATCH_TILE
    n_e, nb_tot = cs_all.shape
    cs_ext = jnp.concatenate([cs_all, jnp.full((n_e, 1), cap_tot, jnp.int32)], axis=1)
    step = GATHER_BLOCK // R
    start = cs_ext[:, 0:nb_tot:step]
    end = cs_ext[:, step::step]
    lo = jnp.arange(cap_tot // R, dtype=jnp.int32) * R
    blo = jnp.sum((end[:, :, None] <= lo[None, None, :]).astype(jnp.int32), axis=1)
    nblk = jnp.sum((start[:, :, None] < lo[None, None, :] + R).astype(jnp.int32), axis=1) - blo
    c0, c1 = cs_ext[:, :-1], cs_ext[:, 1:]
    j0 = c0 // R
    cnt = jnp.where(c1 > c0, (c1 - 1) // R - j0 + 1, 0)
    return blo.reshape(-1), nblk.reshape(-1), j0, cnt


def _trunk_front(x, seq_tables, p):
    B, S, D = x.shape
    T = B * S
    x2d = x.reshape(T, D)
    rc, rsa, rsb = seq_tables
    qs, ff, fb, v, gs, qa, kv = _in_proj(x2d, p["g0"], p["b0"], p["w_in"], p["lbf"], p["lbb"],
                                             rc, rsa, rsb, S)
    of = _hgrn(qs, ff, v, B, S, reverse=False)
    ob = _hgrn(qs, fb, v, B, S, reverse=True)
    att = _attn(qa, kv, p["sink"], B, S)
    return _mix(of, ob, gs, att, x2d, p["g0"], p["b0"], p["ng"], p["w_out"], p["g1"], p["b1"], p["w_router_t"])


def kernel(x_prompt, x_sample, emb_ln_g, emb_ln_b, w_in, hgrn_lb_fwd, hgrn_lb_bwd, hgrn_norm_g, attn_sink, w_out, ln1_g, ln1_b, w_router, w_gate, w_up, w_down, ln2_g, ln2_b):
    assert w_in.shape[0] == 1, "single-layer problem"
    R = DISPATCH_TILE
    p = dict(
        g0=emb_ln_g.reshape(1, D_MODEL), b0=emb_ln_b.reshape(1, D_MODEL),
        w_in=w_in[0].astype(BF16), lbf=hgrn_lb_fwd, lbb=hgrn_lb_bwd,
        ng=hgrn_norm_g[0].reshape(1, REC_DK), sink=attn_sink[0],
        w_out=w_out[0].astype(BF16), g1=ln1_g[0].reshape(1, D_MODEL), b1=ln1_b[0].reshape(1, D_MODEL),
        w_router_t=w_router[0].T,
    )
    g2, b2 = ln2_g[0].reshape(1, D_MODEL), ln2_b[0].reshape(1, D_MODEL)
    wg, wu, wd = w_gate[0].astype(BF16), w_up[0].astype(BF16), w_down[0].astype(BF16)

    groups = []
    slot_offset = 0
    for x in (x_prompt, x_sample):
        B, S, _ = x.shape
        T = B * S
        cap = CAPACITY_FACTOR * T // N_EXPERTS
        assert T % GATHER_BLOCK == 0 and cap % R == 0
        x1, x1b, aff = _trunk_front(x, _rotary_tables(S), p)
        aff_blocks = aff.reshape(N_EXPERTS, T // R, R)
        pos, cs = _select(aff_blocks, cap, slot_offset)
        groups.append(dict(x=x, x1=x1, x1b=x1b, aff=aff_blocks, pos=pos, cs=cs[:, :, 0]))
        slot_offset += cap
    cap_tot = slot_offset

    pos_all = jnp.concatenate([g["pos"] for g in groups], axis=1)
    aff_all = jnp.concatenate([g["aff"] for g in groups], axis=1)
    cs_all = jnp.concatenate([g["cs"] for g in groups], axis=1)
    blo, nblk, j0, cnt = _dispatch_plan(cs_all, cap_tot)
    pos_gather = pos_all.reshape(N_EXPERTS, -1, GATHER_BLOCK)
    y_all = _experts(blo, nblk, groups[0]["x1b"], groups[1]["x1b"], pos_gather, wg, wu, wd, cap_tot // R)

    pos_blocks = jnp.transpose(pos_all, (1, 0, 2))
    aff_blocks = jnp.transpose(aff_all, (1, 0, 2))
    outs = []
    block_offset = 0
    for g in groups:
        nb = g["x1"].shape[0] // R
        sl = slice(block_offset, block_offset + nb)
        outs.append(_combine(j0[:, sl].reshape(-1), cnt[:, sl].reshape(-1), g["x1"], pos_blocks, aff_blocks,
                             y_all, g2, b2, block_offset).reshape(g["x"].shape))
        block_offset += nb
    return tuple(outs)
```

```python
import functools

import jax
import jax.numpy as jnp
import numpy as np
from jax import lax
from jax.experimental import pallas as pl
from jax.experimental.pallas import tpu as pltpu

D_MODEL = 1024
REC_WIDTH = 512
ATT_WIDTH = 512
REC_HEADS = 4
REC_DK = 128
REC_CHUNK = 64
ATT_HEADS = 8
ATT_KV_HEADS = 2
ATT_HEAD_DIM = 64
ATT_GROUP = 4
WINDOW = 128
ATT_BLOCK = 128
ROT_DIM = 16
ROPE_THETA = 500000.0
N_EXPERTS = 16
EXPERT_FF = 2048
CAPACITY_FACTOR = 2
NORM_EPS = 1e-5
DEEPNORM_ALPHA = 2.0 ** 0.25
KV_WIDTH = ATT_KV_HEADS * ATT_HEAD_DIM
IN_PROJ_WIDTH = 5 * REC_WIDTH + ATT_WIDTH + 2 * KV_WIDTH

TOKEN_TILE = 512
SEQ_TILE = 512
DISPATCH_TILE = 256
COMBINE_WINDOW = 128
EXPERT_TILE = 512
FF_CHUNK = 512
VMEM_LIMIT_BYTES = 56 * 1024 * 1024

F32 = jnp.float32
BF16 = jnp.bfloat16
HIGHEST = lax.Precision.HIGHEST


def _params(n_axes):
    return pltpu.CompilerParams(dimension_semantics=("arbitrary",) * n_axes,
                                vmem_limit_bytes=VMEM_LIMIT_BYTES)


def _resident(shape):
    nd = len(shape)
    return pl.BlockSpec(shape, lambda *_: (0,) * nd, pipeline_mode=pl.Buffered(1))


def _layer_norm(x, g, b):
    mu = jnp.mean(x, axis=-1, keepdims=True)
    xc = x - mu
    var = jnp.mean(xc * xc, axis=-1, keepdims=True)
    return xc * lax.rsqrt(var + NORM_EPS) * g + b


def _lower_bound(p):
    p0, p1 = p[0:1, :], p[1:2, :]
    m = jnp.maximum(p0, p1)
    e0, e1 = jnp.exp(p0 - m), jnp.exp(p1 - m)
    return e0 / (e0 + e1)


def _dot(a, b, dims=((1,), (0,)), precision=None):
    return lax.dot_general(a, b, (dims, ((), ())), preferred_element_type=F32, precision=precision)


def _in_proj_kernel(x_ref, g_ref, b_ref, w_ref, lbf_ref, lbb_ref, rc_ref, rsa_ref, rsb_ref,
                    qs_ref, ff_ref, fb_ref, v_ref, gs_ref, qa_ref, kv_ref):
    xn = _layer_norm(x_ref[...], g_ref[...], b_ref[...])
    proj = _dot(xn.astype(BF16), w_ref[...])
    W = REC_WIDTH
    q_r, zf, zb = proj[:, 0:W], proj[:, W:2 * W], proj[:, 2 * W:3 * W]
    i_r, g_r = proj[:, 3 * W:4 * W], proj[:, 4 * W:5 * W]
    q_a = proj[:, 5 * W:5 * W + ATT_WIDTH]
    k_a = proj[:, 5 * W + ATT_WIDTH:5 * W + ATT_WIDTH + KV_WIDTH]
    v_a = proj[:, 5 * W + ATT_WIDTH + KV_WIDTH:]
    lbf = _lower_bound(lbf_ref[...])
    lbb = _lower_bound(lbb_ref[...])
    qs_ref[...] = q_r * jax.nn.sigmoid(q_r)
    ff_ref[...] = lbf + (1.0 - lbf) * jax.nn.sigmoid(zf)
    fb_ref[...] = lbb + (1.0 - lbb) * jax.nn.sigmoid(zb)
    v_ref[...] = i_r.astype(BF16)
    gs_ref[...] = g_r * jax.nn.sigmoid(g_r)
    rc, rsa, rsb = rc_ref[...], rsa_ref[...], rsb_ref[...]
    k_rot = k_a * rc + pltpu.roll(k_a, KV_WIDTH - ROT_DIM // 2, 1) * rsa + pltpu.roll(k_a, ROT_DIM // 2, 1) * rsb
    half_kv = KV_WIDTH // 2
    kv_ref[:, 0:KV_WIDTH] = k_rot.astype(BF16)
    kv_ref[:, KV_WIDTH:2 * KV_WIDTH] = pltpu.roll(k_rot, half_kv, 1).astype(BF16)
    kv_ref[:, 2 * KV_WIDTH:3 * KV_WIDTH] = v_a.astype(BF16)
    kv_ref[:, 3 * KV_WIDTH:] = pltpu.roll(v_a, half_kv, 1).astype(BF16)
    n_rep = ATT_WIDTH // KV_WIDTH
    rc4 = jnp.concatenate([rc] * n_rep, axis=1)
    rsa4 = jnp.concatenate([rsa] * n_rep, axis=1)
    rsb4 = jnp.concatenate([rsb] * n_rep, axis=1)
    q_rot = q_a * rc4 + pltpu.roll(q_a, ATT_WIDTH - ROT_DIM // 2, 1) * rsa4 + pltpu.roll(q_a, ROT_DIM // 2, 1) * rsb4
    qa_ref[...] = (q_rot * (ATT_HEAD_DIM ** -0.5)).astype(BF16)


def _in_proj(x2d, g, b, w_in_bf16, lbf, lbb, rc, rsa, rsb, seq_len):
    T = x2d.shape[0]
    tm = TOKEN_TILE
    assert T % tm == 0 and seq_len % tm == 0
    n_seq = seq_len // tm
    row = lambda i: (i, 0)
    pos_row = lambda i: (i % n_seq, 0)
    wide = lambda w: pl.BlockSpec((tm, w), row)
    out_shapes = (
        jax.ShapeDtypeStruct((T, REC_WIDTH), F32),
        jax.ShapeDtypeStruct((T, REC_WIDTH), F32),
        jax.ShapeDtypeStruct((T, REC_WIDTH), F32),
        jax.ShapeDtypeStruct((T, REC_WIDTH), BF16),
        jax.ShapeDtypeStruct((T, REC_WIDTH), F32),
        jax.ShapeDtypeStruct((T, ATT_WIDTH), BF16),
        jax.ShapeDtypeStruct((T, 4 * KV_WIDTH), BF16),
    )
    return pl.pallas_call(
        _in_proj_kernel,
        grid=(T // tm,),
        in_specs=[wide(D_MODEL), _resident((1, D_MODEL)), _resident((1, D_MODEL)),
                  _resident((D_MODEL, IN_PROJ_WIDTH)), _resident((2, REC_WIDTH)), _resident((2, REC_WIDTH)),
                  pl.BlockSpec((tm, KV_WIDTH), pos_row), pl.BlockSpec((tm, KV_WIDTH), pos_row),
                  pl.BlockSpec((tm, KV_WIDTH), pos_row)],
        out_specs=[wide(REC_WIDTH)] * 5 + [wide(ATT_WIDTH), wide(4 * KV_WIDTH)],
        out_shape=out_shapes,
        compiler_params=_params(1),
        name="in_proj",
    )(x2d, g, b, w_in_bf16, lbf, lbb, rc, rsa, rsb)


def _hgrn_kernel(q_ref, f_ref, v_ref, o_ref, state_ref, *, reverse):
    C = REC_CHUNK
    n_chunks = q_ref.shape[0] // C

    @pl.when(pl.program_id(1) == 0)
    def _():
        state_ref[...] = jnp.zeros_like(state_ref)

    t_i = lax.broadcasted_iota(jnp.int32, (C, C), 0)
    s_i = lax.broadcasted_iota(jnp.int32, (C, C), 1)
    causal = (s_i >= t_i) if reverse else (s_i <= t_i)
    csum = causal.astype(F32)
    mid = C // 2 if reverse else C // 2 - 1
    last = 0 if reverse else C - 1

    def chunk(ci, carry):
        c = (n_chunks - 1 - ci) if reverse else ci
        r0 = pl.multiple_of(c * C, C)
        f = f_ref[pl.ds(r0, C), :]
        q = q_ref[pl.ds(r0, C), :]
        v = v_ref[pl.ds(r0, C), :]
        kk = 1.0 - f
        b = _dot(csum, jnp.log(f), precision=HIGHEST)
        b_mid = b[mid:mid + 1, :]
        b_last = b[last:last + 1, :]
        qe = (q * jnp.exp(b - b_mid)).astype(BF16)
        ke = (kk * jnp.exp(b_mid - b)).astype(BF16)
        ku = (kk * jnp.exp(b_last - b)).astype(BF16)
        qb = (q * jnp.exp(b)).astype(BF16)
        decay = jnp.exp(b_last)
        outs = []
        for h in range(REC_HEADS):
            sl = slice(h * REC_DK, (h + 1) * REC_DK)
            a = _dot(qe[:, sl], ke[:, sl], ((1,), (1,)))
            a = jnp.where(causal, a, 0.0).astype(BF16)
            st = state_ref[h]
            o = _dot(a, v[:, sl]) + _dot(qb[:, sl], st.astype(BF16), ((1,), (1,)))
            u_t = _dot(v[:, sl], ku[:, sl], ((0,), (0,)))
            state_ref[h] = st * decay[:, sl] + u_t
            outs.append(o)
        o_ref[pl.ds(r0, C), :] = jnp.concatenate(outs, axis=1)
        return carry

    lax.fori_loop(0, n_chunks, chunk, 0, unroll=True)


def _hgrn(qs, f, v, batch, seq_len, reverse):
    T = qs.shape[0]
    tc = SEQ_TILE
    assert seq_len % tc == 0
    n_s = seq_len // tc
    if reverse:
        idx = lambda b, s: (b * n_s + (n_s - 1 - s), 0)
    else:
        idx = lambda b, s: (b * n_s + s, 0)
    spec = pl.BlockSpec((tc, REC_WIDTH), idx)
    return pl.pallas_call(
        functools.partial(_hgrn_kernel, reverse=reverse),
        grid=(batch, n_s),
        in_specs=[spec, spec, spec],
        out_specs=spec,
        out_shape=jax.ShapeDtypeStruct((T, REC_WIDTH), F32),
        scratch_shapes=[pltpu.VMEM((REC_HEADS, REC_DK, REC_DK), F32)],
        compiler_params=_params(2),
        name="hgrn_bwd" if reverse else "hgrn_fwd",
    )(qs, f, v)


ATT_ROW_CHUNK = 32


def _attn_kernel(sink_ref, q_ref, kvp_ref, kvc_ref, kvn_ref, o_ref, s_ref, p_ref, bias_ref, *, seq_len):
    blk = pl.program_id(1)
    n_q, n_k = ATT_BLOCK, 3 * ATT_BLOCK
    qi = lax.broadcasted_iota(jnp.int32, (n_q, n_k), 0)
    kj = lax.broadcasted_iota(jnp.int32, (n_q, n_k), 1)
    kpos = blk * ATT_BLOCK - ATT_BLOCK + kj
    valid = (jnp.abs(kj - ATT_BLOCK - qi) <= WINDOW) & (kpos >= 0) & (kpos < seq_len)
    bias_ref[...] = jnp.where(valid, 0.0, -1e30)
    band = jnp.concatenate([kvp_ref[...], kvc_ref[...], kvn_ref[...]], axis=0)
    low = lax.broadcasted_iota(jnp.int32, (n_k, KV_WIDTH), 1) < ATT_HEAD_DIM
    zero = jnp.zeros((n_k, KV_WIDTH), BF16)
    k_plain, k_swap = band[:, 0:KV_WIDTH], band[:, KV_WIDTH:2 * KV_WIDTH]
    v_plain, v_swap = band[:, 2 * KV_WIDTH:3 * KV_WIDTH], band[:, 3 * KV_WIDTH:]
    slab = 2 * ATT_HEAD_DIM
    for g in range(ATT_KV_HEADS):
        src_lo, src_hi = (k_plain, k_swap) if g == 0 else (k_swap, k_plain)
        vsrc_lo, vsrc_hi = (v_plain, v_swap) if g == 0 else (v_swap, v_plain)
        k_lo, k_hi = jnp.where(low, src_lo, zero), jnp.where(low, zero, src_hi)
        v_lo, v_hi = jnp.where(low, vsrc_lo, zero), jnp.where(low, zero, vsrc_hi)
        q2 = jnp.concatenate([q_ref[:, (2 * g) * slab:(2 * g + 1) * slab],
                              q_ref[:, (2 * g + 1) * slab:(2 * g + 2) * slab]], axis=0)
        s_ref[0:2 * n_q, :] = _dot(q2, k_lo, ((1,), (1,)))
        s_ref[2 * n_q:4 * n_q, :] = _dot(q2, k_hi, ((1,), (1,)))
        heads = (4 * g, 4 * g + 2, 4 * g + 1, 4 * g + 3)
        for r in range(4 * n_q // ATT_ROW_CHUNK):
            r0 = r * ATT_ROW_CHUNK
            q0 = r0 % n_q
            sink = sink_ref[heads[r0 // n_q]]
            s = s_ref[r0:r0 + ATT_ROW_CHUNK, :] + bias_ref[q0:q0 + ATT_ROW_CHUNK, :]
            m = jnp.maximum(jnp.max(s, axis=-1, keepdims=True), sink)
            p = jnp.exp(s - m)
            den = jnp.sum(p, axis=-1, keepdims=True) + jnp.exp(sink - m)
            p_ref[r0:r0 + ATT_ROW_CHUNK, :] = (p / den).astype(BF16)
        for i in range(2):
            lo_rows = p_ref[i * n_q:(i + 1) * n_q, :]
            hi_rows = p_ref[(2 + i) * n_q:(3 + i) * n_q, :]
            o = _dot(lo_rows, v_lo) + _dot(hi_rows, v_hi)
            o_ref[:, (2 * g + i) * slab:(2 * g + i + 1) * slab] = o.astype(o_ref.dtype)


def _attn(qa, kv, sink, batch, seq_len):
    T = qa.shape[0]
    n_b = seq_len // ATT_BLOCK
    cur = lambda b, i: (b * n_b + i, 0)
    prev = lambda b, i: (b * n_b + jnp.maximum(i - 1, 0), 0)
    nxt = lambda b, i: (b * n_b + jnp.minimum(i + 1, n_b - 1), 0)
    kvs = lambda im: pl.BlockSpec((ATT_BLOCK, 4 * KV_WIDTH), im)
    n_rows = ATT_GROUP * ATT_BLOCK
    return pl.pallas_call(
        functools.partial(_attn_kernel, seq_len=seq_len),
        grid=(batch, n_b),
        in_specs=[pl.BlockSpec(memory_space=pltpu.SMEM),
                  pl.BlockSpec((ATT_BLOCK, ATT_WIDTH), cur), kvs(prev), kvs(cur), kvs(nxt)],
        out_specs=pl.BlockSpec((ATT_BLOCK, ATT_WIDTH), cur),
        out_shape=jax.ShapeDtypeStruct((T, ATT_WIDTH), BF16),
        scratch_shapes=[pltpu.VMEM((n_rows, 3 * ATT_BLOCK), F32), pltpu.VMEM((n_rows, 3 * ATT_BLOCK), BF16),
                        pltpu.VMEM((ATT_BLOCK, 3 * ATT_BLOCK), F32)],
        compiler_params=_params(2),
        name="attn",
    )(sink, qa, kv, kv, kv)


def _mix_kernel(of_ref, ob_ref, gs_ref, att_ref, x_ref, g0_ref, b0_ref, ng_ref, wo_ref,
                g1_ref, b1_ref, wr_ref, x1_ref, x1b_ref, aff_ref):
    o_rec = of_ref[...] + ob_ref[...]
    gs = gs_ref[...]
    ng = ng_ref[...]
    recs = []
    for h in range(REC_HEADS):
        sl = slice(h * REC_DK, (h + 1) * REC_DK)
        oh = o_rec[:, sl]
        ms = jnp.mean(oh * oh, axis=-1, keepdims=True)
        recs.append(oh * lax.rsqrt(ms + NORM_EPS) * ng * gs[:, sl])
    rec = jnp.concatenate(recs, axis=1).astype(BF16)
    mix = _dot(rec, wo_ref[0:REC_WIDTH, :]) + _dot(att_ref[...], wo_ref[REC_WIDTH:, :])
    x0 = _layer_norm(x_ref[...], g0_ref[...], b0_ref[...])
    x1 = _layer_norm(DEEPNORM_ALPHA * x0 + mix, g1_ref[...], b1_ref[...])
    x1_ref[...] = x1
    x1b_ref[...] = x1.astype(BF16)
    logits = _dot(wr_ref[...], x1, ((1,), (1,)), precision=HIGHEST)
    m = jnp.max(logits, axis=0, keepdims=True)
    e = jnp.exp(logits - m)
    aff_ref[...] = e / jnp.sum(e, axis=0, keepdims=True)


def _mix(of, ob, gs, att, x2d, g0, b0, ng, w_out_bf16, g1, b1, w_router_t):
    T = x2d.shape[0]
    tm = TOKEN_TILE
    row = lambda i: (i, 0)
    wide = lambda w: pl.BlockSpec((tm, w), row)
    vec = _resident((1, D_MODEL))
    return pl.pallas_call(
        _mix_kernel,
        grid=(T // tm,),
        in_specs=[wide(REC_WIDTH), wide(REC_WIDTH), wide(REC_WIDTH), wide(ATT_WIDTH), wide(D_MODEL),
                  vec, vec, _resident((1, REC_DK)), _resident((D_MODEL, D_MODEL)), vec, vec,
                  _resident((N_EXPERTS, D_MODEL))],
        out_specs=[wide(D_MODEL), wide(D_MODEL), pl.BlockSpec((N_EXPERTS, tm), lambda i: (0, i))],
        out_shape=(jax.ShapeDtypeStruct((T, D_MODEL), F32),
                   jax.ShapeDtypeStruct((T, D_MODEL), BF16),
                   jax.ShapeDtypeStruct((N_EXPERTS, T), F32)),
        compiler_params=_params(1),
        name="mix",
    )(of, ob, gs, att, x2d, g0, b0, ng, w_out_bf16, g1, b1, w_router_t)


def _select_kernel(aff_ref, pos_ref, cs_ref, *, cap, slot_offset):
    aff = aff_ref[0]
    nb, w = aff.shape
    bits = pltpu.bitcast(aff, jnp.int32)
    capf = jnp.float32(cap)

    def total(mask):
        x = mask.astype(F32)
        return jnp.sum(jnp.sum(x, axis=0, keepdims=True), axis=1, keepdims=True)

    def search(i, prefix):
        cand = prefix | lax.shift_left(jnp.int32(1), 30 - i)
        return jnp.where(total(bits >= cand) >= capf, cand, prefix)

    thr = lax.fori_loop(0, 31, search, jnp.zeros((1, 1), jnp.int32))

    c_i = lax.broadcasted_iota(jnp.int32, (w, w), 0)
    c_j = lax.broadcasted_iota(jnp.int32, (w, w), 1)
    upper = (c_i <= c_j).astype(BF16)
    r_i = lax.broadcasted_iota(jnp.int32, (nb, nb), 0)
    r_j = lax.broadcasted_iota(jnp.int32, (nb, nb), 1)
    strict_lower = (r_j < r_i).astype(BF16)

    def prefix_counts(mask):
        x = mask.astype(F32)
        inc = _dot(x.astype(BF16), upper)
        tot = jnp.broadcast_to(inc[:, w - 1:w], (nb, 128)).astype(BF16)
        before = _dot(strict_lower, tot)
        return inc - x + before[:, 0:1], before

    gt = bits > thr
    eq = bits == thr
    need = capf - total(gt)
    eq_rank, _ = prefix_counts(eq)
    sel = gt | (eq & (eq_rank < need))
    rank, before = prefix_counts(sel)
    pos_ref[0] = jnp.where(sel, rank.astype(jnp.int32) + slot_offset, -1)
    cs_ref[0] = before.astype(jnp.int32) + slot_offset


def _select(aff_blocks, cap, slot_offset):
    n_e, nb, w = aff_blocks.shape
    spec = pl.BlockSpec((1, nb, w), lambda e: (e, 0, 0))
    return pl.pallas_call(
        functools.partial(_select_kernel, cap=cap, slot_offset=slot_offset),
        grid=(n_e,),
        in_specs=[spec],
        out_specs=[spec, pl.BlockSpec((1, nb, 128), lambda e: (e, 0, 0))],
        out_shape=(jax.ShapeDtypeStruct((n_e, nb, w), jnp.int32),
                   jax.ShapeDtypeStruct((n_e, nb, 128), jnp.int32)),
        compiler_params=_params(1),
        name="select",
    )(aff_blocks)


def _dispatch_kernel(c_ref, xa_ref, xb_ref, pos_ref, xd_hbm, stage, onehot_ref, outbuf, sem, nflushed,
                     *, nb_first, nb_tot):
    i = pl.program_id(0)
    R, W = DISPATCH_TILE, COMBINE_WINDOW
    stride = nb_tot + 1
    group = 4

    @pl.when(i == 0)
    def _():
        stage[...] = jnp.zeros_like(stage)
        nflushed[0] = 0

    x = jnp.where(i < nb_first, xa_ref[...], xb_ref[...])
    slot_iota = lax.broadcasted_iota(jnp.int32, (W, R), 0)

    def out_copy(slot, e, row):
        return pltpu.make_async_copy(outbuf.at[slot], xd_hbm.at[e, pl.ds(row, R), :], sem.at[slot])

    def onehot(e, first):
        return jnp.where(pos_ref[0, e:e + 1, :] == slot_iota + first, 1.0, 0.0).astype(BF16)

    for g0 in range(0, N_EXPERTS, group):
        for e in range(g0, g0 + group):
            w0 = (c_ref[e * stride + i] // (W // 2)) * (W // 2)
            onehot_ref[(e - g0) * W:(e - g0 + 1) * W, :] = onehot(e, w0)
        rows = _dot(onehot_ref[...], x).astype(BF16)
        for e in range(g0, g0 + group):
            c0, c1 = c_ref[e * stride + i], c_ref[e * stride + i + 1]
            base = (c0 // R) * R
            w0 = (c0 // (W // 2)) * (W // 2)
            off = pl.multiple_of(w0 - base, W // 2)
            stage[e, pl.ds(off, W), :] += rows[(e - g0) * W:(e - g0 + 1) * W, :]

            def window(w, carry, e=e, base=base, w0=w0):
                first = w0 + w * W
                o2 = pl.multiple_of(first - base, W // 2)
                stage[e, pl.ds(o2, W), :] += _dot(onehot(e, first), x).astype(BF16)
                return carry

            lax.fori_loop(1, jnp.where(c1 > c0, (c1 - w0 + W - 1) // W, 0), window, 0)

            def flush(k, carry, e=e, c0=c0):
                f = nflushed[0]
                slot = f % 2

                @pl.when(f >= 2)
                def _():
                    out_copy(slot, 0, 0).wait()

                outbuf[slot] = stage[e, 0:R, :]
                out_copy(slot, e, pl.multiple_of((c0 // R + k) * R, R)).start()
                stage[e, 0:R, :] = stage[e, R:2 * R, :]
                stage[e, R:2 * R, :] = jnp.zeros((R, D_MODEL), BF16)
                nflushed[0] = f + 1
                return carry

            lax.fori_loop(0, c1 // R - c0 // R, flush, 0)

    @pl.when(i == nb_tot - 1)
    def _():
        out_copy(0, 0, 0).wait()
        out_copy(1, 0, 0).wait()


def _dispatch(cs_ext_flat, x1b_a, x1b_b, pos_blocks, cap_tot):
    nb_tot, n_e, R = pos_blocks.shape
    W = COMBINE_WINDOW
    nb_first = x1b_a.shape[0] // R
    assert n_e * (cap_tot // R) >= 2
    grid_spec = pltpu.PrefetchScalarGridSpec(
        num_scalar_prefetch=1,
        grid=(nb_tot,),
        in_specs=[pl.BlockSpec((R, D_MODEL), lambda i, *_: (jnp.minimum(i, nb_first - 1), 0)),
                  pl.BlockSpec((R, D_MODEL), lambda i, *_: (jnp.maximum(i - nb_first, 0), 0)),
                  pl.BlockSpec((1, n_e, R), lambda i, *_: (i, 0, 0))],
        out_specs=pl.BlockSpec(memory_space=pl.ANY),
        scratch_shapes=[pltpu.VMEM((n_e, 2 * R + W, D_MODEL), BF16),
                        pltpu.VMEM((4 * W, R), BF16),
                        pltpu.VMEM((2, R, D_MODEL), BF16),
                        pltpu.SemaphoreType.DMA((2,)),
                        pltpu.SMEM((1,), jnp.int32)],
    )
    return pl.pallas_call(
        functools.partial(_dispatch_kernel, nb_first=nb_first, nb_tot=nb_tot),
        grid_spec=grid_spec,
        out_shape=jax.ShapeDtypeStruct((n_e, cap_tot, D_MODEL), BF16),
        compiler_params=_params(1),
        name="dispatch",
    )(cs_ext_flat, x1b_a, x1b_b, pos_blocks)


def _expert_kernel(x_ref, wg_ref, wu_ref, wd_ref, y_ref):
    xg = x_ref[0]
    y = jnp.zeros((xg.shape[0], D_MODEL), F32)
    for c in range(EXPERT_FF // FF_CHUNK):
        cs = slice(c * FF_CHUNK, (c + 1) * FF_CHUNK)
        h1 = _dot(xg, wg_ref[0, :, cs])
        h2 = _dot(xg, wu_ref[0, :, cs])
        h = (h1 * jax.nn.sigmoid(h1) * h2).astype(BF16)
        y = y + _dot(h, wd_ref[0, cs, :])
    y_ref[0] = y.astype(y_ref.dtype)


def _experts(xd, wg, wu, wd):
    n_e, cap_tot, _ = xd.shape
    tm = EXPERT_TILE
    assert cap_tot % tm == 0
    rows = pl.BlockSpec((1, tm, D_MODEL), lambda e, j: (e, j, 0))
    return pl.pallas_call(
        _expert_kernel,
        grid=(n_e, cap_tot // tm),
        in_specs=[rows,
                  pl.BlockSpec((1, D_MODEL, EXPERT_FF), lambda e, j: (e, 0, 0)),
                  pl.BlockSpec((1, D_MODEL, EXPERT_FF), lambda e, j: (e, 0, 0)),
                  pl.BlockSpec((1, EXPERT_FF, D_MODEL), lambda e, j: (e, 0, 0))],
        out_specs=rows,
        out_shape=jax.ShapeDtypeStruct((n_e, cap_tot, D_MODEL), BF16),
        compiler_params=_params(2),
        name="experts",
    )(xd, wg, wu, wd)


def _combine_kernel(w0_ref, nwin_ref, x1_ref, pos_ref, aff_ref, y_hbm, g2_ref, b2_ref, o_ref,
                    ymain, yextra, sem_main, sem_extra, gate_ref, acc_ref, *, n_blocks, cap_tot):
    i = pl.program_id(0)
    R, W = DISPATCH_TILE, COMBINE_WINDOW
    half = i % 2

    def main_copy(e, blk, h):
        w0 = pl.multiple_of(w0_ref[e * n_blocks + blk], W // 2)
        return pltpu.make_async_copy(y_hbm.at[e, pl.ds(w0, W), :], ymain.at[h, pl.ds(e * W, W), :],
                                     sem_main.at[h, e])

    def start_main(blk, h):
        for e in range(N_EXPERTS):
            @pl.when(nwin_ref[e * n_blocks + blk] > 0)
            def _(e=e):
                main_copy(e, blk, h).start()

    @pl.when(i == 0)
    def _():
        ymain[...] = jnp.zeros_like(ymain)
        start_main(0, 0)

    @pl.when(i + 1 < n_blocks)
    def _():
        start_main(i + 1, 1 - half)

    slot_iota = lax.broadcasted_iota(jnp.int32, (W, R), 0)

    for e in range(N_EXPERTS):
        slot_ids = slot_iota + w0_ref[e * n_blocks + i]
        gate_ref[e * W:(e + 1) * W, :] = jnp.where(pos_ref[0, e:e + 1, :] == slot_ids,
                                                   aff_ref[0, e:e + 1, :], 0.0).astype(BF16)

    for e in range(N_EXPERTS):
        @pl.when(nwin_ref[e * n_blocks + i] > 0)
        def _(e=e):
            main_copy(e, i, half).wait()

    acc_ref[...] = DEEPNORM_ALPHA * x1_ref[...] + _dot(gate_ref[...], ymain[half], ((0,), (0,)))

    def expert_extras(e, carry):
        w0 = w0_ref[e * n_blocks + i]

        def window(w, c):
            first = w0 + w * W
            ws = pl.multiple_of(jnp.minimum(first, cap_tot - W), W // 2)
            cp = pltpu.make_async_copy(y_hbm.at[e, pl.ds(ws, W), :], yextra, sem_extra)
            cp.start()
            cp.wait()
            slot_ids = slot_iota + ws
            hit = (pos_ref[0, pl.ds(e, 1), :] == slot_ids) & (slot_ids >= first)
            gated = jnp.where(hit, aff_ref[0, pl.ds(e, 1), :], 0.0).astype(BF16)
            acc_ref[...] += _dot(gated, yextra[...], ((0,), (0,)))
            return c

        return lax.fori_loop(1, nwin_ref[e * n_blocks + i], window, carry)

    lax.fori_loop(0, N_EXPERTS, expert_extras, 0)
    o_ref[...] = _layer_norm(acc_ref[...], g2_ref[...], b2_ref[...])


def _combine(w0, nwin, x1, pos_blocks, aff_blocks, y_all, g2, b2, block_offset):
    T = x1.shape[0]
    R, W = DISPATCH_TILE, COMBINE_WINDOW
    n_blocks = T // R
    cap_tot = y_all.shape[1]
    grid_spec = pltpu.PrefetchScalarGridSpec(
        num_scalar_prefetch=2,
        grid=(n_blocks,),
        in_specs=[pl.BlockSpec((R, D_MODEL), lambda i, *_: (i, 0)),
                  pl.BlockSpec((1, N_EXPERTS, R), lambda i, *_: (i + block_offset, 0, 0)),
                  pl.BlockSpec((1, N_EXPERTS, R), lambda i, *_: (i + block_offset, 0, 0)),
                  pl.BlockSpec(memory_space=pl.ANY),
                  pl.BlockSpec((1, D_MODEL), lambda i, *_: (0, 0)),
                  pl.BlockSpec((1, D_MODEL), lambda i, *_: (0, 0))],
        out_specs=pl.BlockSpec((R, D_MODEL), lambda i, *_: (i, 0)),
        scratch_shapes=[pltpu.VMEM((2, N_EXPERTS * W, D_MODEL), BF16),
                        pltpu.VMEM((W, D_MODEL), BF16),
                        pltpu.SemaphoreType.DMA((2, N_EXPERTS)),
                        pltpu.SemaphoreType.DMA(()),
                        pltpu.VMEM((N_EXPERTS * W, R), BF16),
                        pltpu.VMEM((R, D_MODEL), F32)],
    )
    return pl.pallas_call(
        functools.partial(_combine_kernel, n_blocks=n_blocks, cap_tot=cap_tot),
        grid_spec=grid_spec,
        out_shape=jax.ShapeDtypeStruct((T, D_MODEL), F32),
        compiler_params=_params(1),
        name="combine",
    )(w0, nwin, x1, pos_blocks, aff_blocks, y_all, g2, b2)


def _rotary_tables(seq_len):
    half = ROT_DIM // 2
    inv = ROPE_THETA ** (-jnp.arange(half, dtype=F32) / half)
    ang = jnp.arange(seq_len, dtype=F32)[:, None] * inv[None, :]
    cos, sin = jnp.cos(ang), jnp.sin(ang)
    pad = ATT_HEAD_DIM - ROT_DIM
    ones = jnp.ones((seq_len, pad), F32)
    zeros = jnp.zeros((seq_len, pad + half), F32)
    rc = jnp.concatenate([cos, cos, ones], axis=1)
    rsa = jnp.concatenate([-sin, zeros], axis=1)
    rsb = jnp.concatenate([jnp.zeros((seq_len, half), F32), sin, zeros[:, :pad]], axis=1)
    two = lambda t: jnp.concatenate([t, t], axis=1)
    return two(rc), two(rsa), two(rsb)


def _dispatch_plan(cs_all, cap_tot):
    n_e, nb_tot = cs_all.shape
    cs_ext = jnp.concatenate([cs_all, jnp.full((n_e, 1), cap_tot, jnp.int32)], axis=1)
    W = COMBINE_WINDOW
    c0, c1 = cs_ext[:, :-1], cs_ext[:, 1:]
    w0 = jnp.minimum((c0 // (W // 2)) * (W // 2), cap_tot - W)
    nwin = jnp.where(c1 > c0, (c1 - w0 + W - 1) // W, 0)
    return cs_ext.reshape(-1), w0, nwin


def _trunk_front(x, seq_tables, p):
    B, S, D = x.shape
    T = B * S
    x2d = x.reshape(T, D)
    rc, rsa, rsb = seq_tables
    qs, ff, fb, v, gs, qa, kv = _in_proj(x2d, p["g0"], p["b0"], p["w_in"], p["lbf"], p["lbb"],
                                             rc, rsa, rsb, S)
    of = _hgrn(qs, ff, v, B, S, reverse=False)
    ob = _hgrn(qs, fb, v, B, S, reverse=True)
    att = _attn(qa, kv, p["sink"], B, S)
    return _mix(of, ob, gs, att, x2d, p["g0"], p["b0"], p["ng"], p["w_out"], p["g1"], p["b1"], p["w_router_t"])


def kernel(x_prompt, x_sample, emb_ln_g, emb_ln_b, w_in, hgrn_lb_fwd, hgrn_lb_bwd, hgrn_norm_g, attn_sink, w_out, ln1_g, ln1_b, w_router, w_gate, w_up, w_down, ln2_g, ln2_b):
    assert w_in.shape[0] == 1, "single-layer problem"
    R = DISPATCH_TILE
    p = dict(
        g0=emb_ln_g.reshape(1, D_MODEL), b0=emb_ln_b.reshape(1, D_MODEL),
        w_in=w_in[0].astype(BF16), lbf=hgrn_lb_fwd, lbb=hgrn_lb_bwd,
        ng=hgrn_norm_g[0].reshape(1, REC_DK), sink=attn_sink[0],
        w_out=w_out[0].astype(BF16), g1=ln1_g[0].reshape(1, D_MODEL), b1=ln1_b[0].reshape(1, D_MODEL),
        w_router_t=w_router[0].T,
    )
    g2, b2 = ln2_g[0].reshape(1, D_MODEL), ln2_b[0].reshape(1, D_MODEL)
    wg, wu, wd = w_gate[0].astype(BF16), w_up[0].astype(BF16), w_down[0].astype(BF16)

    groups = []
    slot_offset = 0
    for x in (x_prompt, x_sample):
        B, S, _ = x.shape
        T = B * S
        cap = CAPACITY_FACTOR * T // N_EXPERTS
        assert T % TOKEN_TILE == 0 and cap % R == 0
        x1, x1b, aff = _trunk_front(x, _rotary_tables(S), p)
        aff_blocks = aff.reshape(N_EXPERTS, T // R, R)
        pos, cs = _select(aff_blocks, cap, slot_offset)
        groups.append(dict(x=x, x1=x1, x1b=x1b, aff=aff_blocks, pos=pos, cs=cs[:, :, 0]))
        slot_offset += cap
    cap_tot = slot_offset

    pos_all = jnp.concatenate([g["pos"] for g in groups], axis=1)
    aff_all = jnp.concatenate([g["aff"] for g in groups], axis=1)
    cs_all = jnp.concatenate([g["cs"] for g in groups], axis=1)
    cs_ext_flat, w0, nwin = _dispatch_plan(cs_all, cap_tot)
    pos_blocks = jnp.transpose(pos_all, (1, 0, 2))
    aff_blocks = jnp.transpose(aff_all, (1, 0, 2))
    xd = _dispatch(cs_ext_flat, groups[0]["x1b"], groups[1]["x1b"], pos_blocks, cap_tot)
    y_all = _experts(xd, wg, wu, wd)
    outs = []
    block_offset = 0
    for g in groups:
        nb = g["x1"].shape[0] // R
        sl = slice(block_offset, block_offset + nb)
        outs.append(_combine(w0[:, sl].reshape(-1), nwin[:, sl].reshape(-1), g["x1"], pos_blocks, aff_blocks,
                             y_all, g2, b2, block_offset).reshape(g["x"].shape))
        block_offset += nb
    return tuple(outs)
```

```python
import functools

import jax
import jax.numpy as jnp
import numpy as np
from jax import lax
from jax.experimental import pallas as pl
from jax.experimental.pallas import tpu as pltpu

D_MODEL = 1024
REC_WIDTH = 512
ATT_WIDTH = 512
REC_HEADS = 4
REC_DK = 128
REC_CHUNK = 64
ATT_HEADS = 8
ATT_KV_HEADS = 2
ATT_HEAD_DIM = 64
ATT_GROUP = 4
WINDOW = 128
ATT_BLOCK = 128
ROT_DIM = 16
ROPE_THETA = 500000.0
N_EXPERTS = 16
EXPERT_FF = 2048
CAPACITY_FACTOR = 2
NORM_EPS = 1e-5
DEEPNORM_ALPHA = 2.0 ** 0.25
KV_WIDTH = ATT_KV_HEADS * ATT_HEAD_DIM
IN_PROJ_WIDTH = 5 * REC_WIDTH + ATT_WIDTH + 2 * KV_WIDTH

TOKEN_TILE = 512
SEQ_TILE = 512
DISPATCH_TILE = 256
COMBINE_WINDOW = 128
EXPERT_TILE = 512
FF_CHUNK = 512
VMEM_LIMIT_BYTES = 56 * 1024 * 1024

F32 = jnp.float32
BF16 = jnp.bfloat16
HIGHEST = lax.Precision.HIGHEST


def _params(n_axes):
    return pltpu.CompilerParams(dimension_semantics=("arbitrary",) * n_axes,
                                vmem_limit_bytes=VMEM_LIMIT_BYTES)


def _resident(shape):
    nd = len(shape)
    return pl.BlockSpec(shape, lambda *_: (0,) * nd, pipeline_mode=pl.Buffered(1))


def _layer_norm(x, g, b):
    mu = jnp.mean(x, axis=-1, keepdims=True)
    xc = x - mu
    var = jnp.mean(xc * xc, axis=-1, keepdims=True)
    return xc * lax.rsqrt(var + NORM_EPS) * g + b


def _lower_bound(p):
    p0, p1 = p[0:1, :], p[1:2, :]
    m = jnp.maximum(p0, p1)
    e0, e1 = jnp.exp(p0 - m), jnp.exp(p1 - m)
    return e0 / (e0 + e1)


def _dot(a, b, dims=((1,), (0,)), precision=None):
    return lax.dot_general(a, b, (dims, ((), ())), preferred_element_type=F32, precision=precision)


def _in_proj_kernel(x_ref, g_ref, b_ref, w_ref, lbf_ref, lbb_ref, rc_ref, rsa_ref, rsb_ref,
                    qs_ref, ff_ref, fb_ref, v_ref, gs_ref, qa_ref, kv_ref):
    xn = _layer_norm(x_ref[...], g_ref[...], b_ref[...])
    proj = _dot(xn.astype(BF16), w_ref[...])
    W = REC_WIDTH
    q_r, zf, zb = proj[:, 0:W], proj[:, W:2 * W], proj[:, 2 * W:3 * W]
    i_r, g_r = proj[:, 3 * W:4 * W], proj[:, 4 * W:5 * W]
    q_a = proj[:, 5 * W:5 * W + ATT_WIDTH]
    k_a = proj[:, 5 * W + ATT_WIDTH:5 * W + ATT_WIDTH + KV_WIDTH]
    v_a = proj[:, 5 * W + ATT_WIDTH + KV_WIDTH:]
    lbf = _lower_bound(lbf_ref[...])
    lbb = _lower_bound(lbb_ref[...])
    qs_ref[...] = q_r * jax.nn.sigmoid(q_r)
    ff_ref[...] = lbf + (1.0 - lbf) * jax.nn.sigmoid(zf)
    fb_ref[...] = lbb + (1.0 - lbb) * jax.nn.sigmoid(zb)
    v_ref[...] = i_r.astype(BF16)
    gs_ref[...] = g_r * jax.nn.sigmoid(g_r)
    rc, rsa, rsb = rc_ref[...], rsa_ref[...], rsb_ref[...]
    k_rot = k_a * rc + pltpu.roll(k_a, KV_WIDTH - ROT_DIM // 2, 1) * rsa + pltpu.roll(k_a, ROT_DIM // 2, 1) * rsb
    half_kv = KV_WIDTH // 2
    kv_ref[:, 0:KV_WIDTH] = k_rot.astype(BF16)
    kv_ref[:, KV_WIDTH:2 * KV_WIDTH] = pltpu.roll(k_rot, half_kv, 1).astype(BF16)
    kv_ref[:, 2 * KV_WIDTH:3 * KV_WIDTH] = v_a.astype(BF16)
    kv_ref[:, 3 * KV_WIDTH:] = pltpu.roll(v_a, half_kv, 1).astype(BF16)
    n_rep = ATT_WIDTH // KV_WIDTH
    rc4 = jnp.concatenate([rc] * n_rep, axis=1)
    rsa4 = jnp.concatenate([rsa] * n_rep, axis=1)
    rsb4 = jnp.concatenate([rsb] * n_rep, axis=1)
    q_rot = q_a * rc4 + pltpu.roll(q_a, ATT_WIDTH - ROT_DIM // 2, 1) * rsa4 + pltpu.roll(q_a, ROT_DIM // 2, 1) * rsb4
    qa_ref[...] = (q_rot * (ATT_HEAD_DIM ** -0.5)).astype(BF16)


def _in_proj(x2d, g, b, w_in_bf16, lbf, lbb, rc, rsa, rsb, seq_len):
    T = x2d.shape[0]
    tm = TOKEN_TILE
    assert T % tm == 0 and seq_len % tm == 0
    n_seq = seq_len // tm
    row = lambda i: (i, 0)
    pos_row = lambda i: (i % n_seq, 0)
    wide = lambda w: pl.BlockSpec((tm, w), row)
    out_shapes = (
        jax.ShapeDtypeStruct((T, REC_WIDTH), F32),
        jax.ShapeDtypeStruct((T, REC_WIDTH), F32),
        jax.ShapeDtypeStruct((T, REC_WIDTH), F32),
        jax.ShapeDtypeStruct((T, REC_WIDTH), BF16),
        jax.ShapeDtypeStruct((T, REC_WIDTH), F32),
        jax.ShapeDtypeStruct((T, ATT_WIDTH), BF16),
        jax.ShapeDtypeStruct((T, 4 * KV_WIDTH), BF16),
    )
    return pl.pallas_call(
        _in_proj_kernel,
        grid=(T // tm,),
        in_specs=[wide(D_MODEL), _resident((1, D_MODEL)), _resident((1, D_MODEL)),
                  _resident((D_MODEL, IN_PROJ_WIDTH)), _resident((2, REC_WIDTH)), _resident((2, REC_WIDTH)),
                  pl.BlockSpec((tm, KV_WIDTH), pos_row), pl.BlockSpec((tm, KV_WIDTH), pos_row),
                  pl.BlockSpec((tm, KV_WIDTH), pos_row)],
        out_specs=[wide(REC_WIDTH)] * 5 + [wide(ATT_WIDTH), wide(4 * KV_WIDTH)],
        out_shape=out_shapes,
        compiler_params=_params(1),
        name="in_proj",
    )(x2d, g, b, w_in_bf16, lbf, lbb, rc, rsa, rsb)


def _hgrn_kernel(q_ref, f_ref, v_ref, o_ref, state_ref, qe_ref, ke_ref, ku_ref, qb_ref, dec_ref,
                 ut_ref, st_ref, b_ref, a_ref, *, reverse):
    C, W = REC_CHUNK, REC_WIDTH
    n_chunks = q_ref.shape[0] // C
    heads = [slice(h * REC_DK, (h + 1) * REC_DK) for h in range(REC_HEADS)]
    chunk_rows = [slice(c * C, (c + 1) * C) for c in range(n_chunks)]

    @pl.when(pl.program_id(1) == 0)
    def _():
        state_ref[...] = jnp.zeros_like(state_ref)

    t_i = lax.broadcasted_iota(jnp.int32, (C, C), 0)
    s_i = lax.broadcasted_iota(jnp.int32, (C, C), 1)
    csum = jnp.where((s_i >= t_i) if reverse else (s_i <= t_i), 1.0, 0.0).astype(BF16)
    mid = C // 2 if reverse else C // 2 - 1
    last = 0 if reverse else C - 1

    for rows in chunk_rows:
        lf = jnp.log(f_ref[rows, :])
        hi = lf.astype(BF16)
        r1 = lf - hi.astype(F32)
        md = r1.astype(BF16)
        lo = (r1 - md.astype(F32)).astype(BF16)
        b3 = _dot(csum, jnp.concatenate([hi, md, lo], axis=1))
        b_ref[rows, :] = b3[:, 0:W] + b3[:, W:2 * W] + b3[:, 2 * W:]
    for c, rows in enumerate(chunk_rows):
        f = f_ref[rows, :]
        q = q_ref[rows, :]
        b = b_ref[rows, :]
        b_mid = b[mid:mid + 1, :]
        b_last = b[last:last + 1, :]
        kk = 1.0 - f
        qe = q * jnp.exp(b - b_mid)
        ke = kk * jnp.exp(b_mid - b)
        qe_ref[rows, :] = qe.astype(BF16)
        ke_ref[rows, :] = ke.astype(BF16)
        qb_ref[rows, :] = (qe * jnp.exp(b_mid)).astype(BF16)
        ku_ref[rows, :] = (ke * jnp.exp(b_last - b_mid)).astype(BF16)
        dec_ref[c:c + 1, :] = jnp.exp(b_last)

    P = 2 * C
    p_t = lax.broadcasted_iota(jnp.int32, (P, P), 0)
    p_s = lax.broadcasted_iota(jnp.int32, (P, P), 1)
    same_chunk = lax.shift_right_logical(p_t, 6) == lax.shift_right_logical(p_s, 6)
    pair_mask = same_chunk & ((p_s >= p_t) if reverse else (p_s <= p_t))
    assert C == 64 and n_chunks % 2 == 0
    pair_rows = [slice(p * P, (p + 1) * P) for p in range(n_chunks // 2)]
    for p, rows in enumerate(pair_rows):
        for h, sl in enumerate(heads):
            a = _dot(qe_ref[rows, sl], ke_ref[rows, sl], ((1,), (1,)))
            a_ref[p, h] = jnp.where(pair_mask, a, 0.0).astype(BF16)
    for p, rows in enumerate(pair_rows):
        for h, sl in enumerate(heads):
            o_ref[rows, sl] = _dot(a_ref[p, h], v_ref[rows, sl])

    for c, rows in enumerate(chunk_rows):
        for h, sl in enumerate(heads):
            ut_ref[h, c] = _dot(v_ref[rows, sl], ku_ref[rows, sl], ((0,), (0,)))

    order = list(reversed(range(n_chunks))) if reverse else list(range(n_chunks))
    for h, sl in enumerate(heads):
        st = state_ref[h]
        for c in order:
            st_ref[h, c] = st.astype(BF16)
            st = st * dec_ref[c:c + 1, sl] + ut_ref[h, c]
        state_ref[h] = st

    for c, rows in enumerate(chunk_rows):
        for h, sl in enumerate(heads):
            o_ref[rows, sl] += _dot(qb_ref[rows, sl], st_ref[h, c], ((1,), (1,)))


def _hgrn(qs, f, v, batch, seq_len, reverse):
    T = qs.shape[0]
    tc = SEQ_TILE
    assert seq_len % tc == 0
    n_s = seq_len // tc
    if reverse:
        idx = lambda b, s: (b * n_s + (n_s - 1 - s), 0)
    else:
        idx = lambda b, s: (b * n_s + s, 0)
    spec = pl.BlockSpec((tc, REC_WIDTH), idx)
    return pl.pallas_call(
        functools.partial(_hgrn_kernel, reverse=reverse),
        grid=(batch, n_s),
        in_specs=[spec, spec, spec],
        out_specs=spec,
        out_shape=jax.ShapeDtypeStruct((T, REC_WIDTH), F32),
        scratch_shapes=[pltpu.VMEM((REC_HEADS, REC_DK, REC_DK), F32)]
        + [pltpu.VMEM((tc, REC_WIDTH), BF16)] * 4
        + [pltpu.VMEM((tc // REC_CHUNK, REC_WIDTH), F32),
           pltpu.VMEM((REC_HEADS, tc // REC_CHUNK, REC_DK, REC_DK), F32),
           pltpu.VMEM((REC_HEADS, tc // REC_CHUNK, REC_DK, REC_DK), BF16),
           pltpu.VMEM((tc, REC_WIDTH), F32),
           pltpu.VMEM((tc // (2 * REC_CHUNK), REC_HEADS, 2 * REC_CHUNK, 2 * REC_CHUNK), BF16)],
        compiler_params=_params(2),
        name="hgrn_bwd" if reverse else "hgrn_fwd",
    )(qs, f, v)


ATT_ROW_CHUNK = 32


def _attn_kernel(sink_ref, q_ref, kvp_ref, kvc_ref, kvn_ref, o_ref, s_ref, p_ref, bias_ref, *, seq_len):
    blk = pl.program_id(1)
    n_q, n_k = ATT_BLOCK, 3 * ATT_BLOCK
    qi = lax.broadcasted_iota(jnp.int32, (n_q, n_k), 0)
    kj = lax.broadcasted_iota(jnp.int32, (n_q, n_k), 1)
    kpos = blk * ATT_BLOCK - ATT_BLOCK + kj
    valid = (jnp.abs(kj - ATT_BLOCK - qi) <= WINDOW) & (kpos >= 0) & (kpos < seq_len)
    bias_ref[...] = jnp.where(valid, 0.0, -1e30)
    band = jnp.concatenate([kvp_ref[...], kvc_ref[...], kvn_ref[...]], axis=0)
    low = lax.broadcasted_iota(jnp.int32, (n_k, KV_WIDTH), 1) < ATT_HEAD_DIM
    zero = jnp.zeros((n_k, KV_WIDTH), BF16)
    k_plain, k_swap = band[:, 0:KV_WIDTH], band[:, KV_WIDTH:2 * KV_WIDTH]
    v_plain, v_swap = band[:, 2 * KV_WIDTH:3 * KV_WIDTH], band[:, 3 * KV_WIDTH:]
    slab = 2 * ATT_HEAD_DIM
    groups = range(ATT_KV_HEADS)
    for g in groups:
        src_lo, src_hi = (k_plain, k_swap) if g == 0 else (k_swap, k_plain)
        k_lo, k_hi = jnp.where(low, src_lo, zero), jnp.where(low, zero, src_hi)
        q2 = jnp.concatenate([q_ref[:, (2 * g) * slab:(2 * g + 1) * slab],
                              q_ref[:, (2 * g + 1) * slab:(2 * g + 2) * slab]], axis=0)
        s_ref[g, 0:2 * n_q, :] = _dot(q2, k_lo, ((1,), (1,)))
        s_ref[g, 2 * n_q:4 * n_q, :] = _dot(q2, k_hi, ((1,), (1,)))
    for g in groups:
        heads = (4 * g, 4 * g + 2, 4 * g + 1, 4 * g + 3)
        for r in range(4 * n_q // ATT_ROW_CHUNK):
            r0 = r * ATT_ROW_CHUNK
            q0 = r0 % n_q
            sink = sink_ref[heads[r0 // n_q]]
            s = s_ref[g, r0:r0 + ATT_ROW_CHUNK, :] + bias_ref[q0:q0 + ATT_ROW_CHUNK, :]
            m = jnp.maximum(jnp.max(s, axis=-1, keepdims=True), sink)
            p = jnp.exp(s - m)
            den = jnp.sum(p, axis=-1, keepdims=True) + jnp.exp(sink - m)
            p_ref[g, r0:r0 + ATT_ROW_CHUNK, :] = (p / den).astype(BF16)
    for g in groups:
        vsrc_lo, vsrc_hi = (v_plain, v_swap) if g == 0 else (v_swap, v_plain)
        v_lo, v_hi = jnp.where(low, vsrc_lo, zero), jnp.where(low, zero, vsrc_hi)
        for i in range(2):
            lo_rows = p_ref[g, i * n_q:(i + 1) * n_q, :]
            hi_rows = p_ref[g, (2 + i) * n_q:(3 + i) * n_q, :]
            o = _dot(lo_rows, v_lo) + _dot(hi_rows, v_hi)
            o_ref[:, (2 * g + i) * slab:(2 * g + i + 1) * slab] = o.astype(o_ref.dtype)


def _attn(qa, kv, sink, batch, seq_len):
    T = qa.shape[0]
    n_b = seq_len // ATT_BLOCK
    cur = lambda b, i: (b * n_b + i, 0)
    prev = lambda b, i: (b * n_b + jnp.maximum(i - 1, 0), 0)
    nxt = lambda b, i: (b * n_b + jnp.minimum(i + 1, n_b - 1), 0)
    kvs = lambda im: pl.BlockSpec((ATT_BLOCK, 4 * KV_WIDTH), im)
    n_rows = ATT_GROUP * ATT_BLOCK
    return pl.pallas_call(
        functools.partial(_attn_kernel, seq_len=seq_len),
        grid=(batch, n_b),
        in_specs=[pl.BlockSpec(memory_space=pltpu.SMEM),
                  pl.BlockSpec((ATT_BLOCK, ATT_WIDTH), cur), kvs(prev), kvs(cur), kvs(nxt)],
        out_specs=pl.BlockSpec((ATT_BLOCK, ATT_WIDTH), cur),
        out_shape=jax.ShapeDtypeStruct((T, ATT_WIDTH), BF16),
        scratch_shapes=[pltpu.VMEM((ATT_KV_HEADS, n_rows, 3 * ATT_BLOCK), F32),
                        pltpu.VMEM((ATT_KV_HEADS, n_rows, 3 * ATT_BLOCK), BF16),
                        pltpu.VMEM((ATT_BLOCK, 3 * ATT_BLOCK), F32)],
        compiler_params=_params(2),
        name="attn",
    )(sink, qa, kv, kv, kv)


def _mix_kernel(of_ref, ob_ref, gs_ref, att_ref, x_ref, g0_ref, b0_ref, ng_ref, wo_ref,
                g1_ref, b1_ref, wr_ref, x1_ref, x1b_ref, aff_ref):
    o_rec = of_ref[...] + ob_ref[...]
    gs = gs_ref[...]
    ng = ng_ref[...]
    recs = []
    for h in range(REC_HEADS):
        sl = slice(h * REC_DK, (h + 1) * REC_DK)
        oh = o_rec[:, sl]
        ms = jnp.mean(oh * oh, axis=-1, keepdims=True)
        recs.append(oh * lax.rsqrt(ms + NORM_EPS) * ng * gs[:, sl])
    rec = jnp.concatenate(recs, axis=1).astype(BF16)
    mix = _dot(rec, wo_ref[0:REC_WIDTH, :]) + _dot(att_ref[...], wo_ref[REC_WIDTH:, :])
    x0 = _layer_norm(x_ref[...], g0_ref[...], b0_ref[...])
    x1 = _layer_norm(DEEPNORM_ALPHA * x0 + mix, g1_ref[...], b1_ref[...])
    x1_ref[...] = x1
    x1b_ref[...] = x1.astype(BF16)
    logits = _dot(wr_ref[...], x1, ((1,), (1,)), precision=HIGHEST)
    m = jnp.max(logits, axis=0, keepdims=True)
    e = jnp.exp(logits - m)
    aff_ref[...] = e / jnp.sum(e, axis=0, keepdims=True)


def _mix(of, ob, gs, att, x2d, g0, b0, ng, w_out_bf16, g1, b1, w_router_t):
    T = x2d.shape[0]
    tm = TOKEN_TILE
    row = lambda i: (i, 0)
    wide = lambda w: pl.BlockSpec((tm, w), row)
    vec = _resident((1, D_MODEL))
    return pl.pallas_call(
        _mix_kernel,
        grid=(T // tm,),
        in_specs=[wide(REC_WIDTH), wide(REC_WIDTH), wide(REC_WIDTH), wide(ATT_WIDTH), wide(D_MODEL),
                  vec, vec, _resident((1, REC_DK)), _resident((D_MODEL, D_MODEL)), vec, vec,
                  _resident((N_EXPERTS, D_MODEL))],
        out_specs=[wide(D_MODEL), wide(D_MODEL), pl.BlockSpec((N_EXPERTS, tm), lambda i: (0, i))],
        out_shape=(jax.ShapeDtypeStruct((T, D_MODEL), F32),
                   jax.ShapeDtypeStruct((T, D_MODEL), BF16),
                   jax.ShapeDtypeStruct((N_EXPERTS, T), F32)),
        compiler_params=_params(1),
        name="mix",
    )(of, ob, gs, att, x2d, g0, b0, ng, w_out_bf16, g1, b1, w_router_t)


def _select_kernel(aff_ref, pos_ref, cs_ref, *, cap, slot_offset):
    aff = aff_ref[0]
    nb, w = aff.shape
    bits = pltpu.bitcast(aff, jnp.int32)
    capf = jnp.float32(cap)

    def total(mask):
        x = mask.astype(F32)
        return jnp.sum(jnp.sum(x, axis=0, keepdims=True), axis=1, keepdims=True)

    def search(i, prefix):
        cand = prefix | lax.shift_left(jnp.int32(1), 30 - i)
        return jnp.where(total(bits >= cand) >= capf, cand, prefix)

    thr = lax.fori_loop(0, 31, search, jnp.zeros((1, 1), jnp.int32))

    c_i = lax.broadcasted_iota(jnp.int32, (w, w), 0)
    c_j = lax.broadcasted_iota(jnp.int32, (w, w), 1)
    upper = (c_i <= c_j).astype(BF16)
    r_i = lax.broadcasted_iota(jnp.int32, (nb, nb), 0)
    r_j = lax.broadcasted_iota(jnp.int32, (nb, nb), 1)
    strict_lower = (r_j < r_i).astype(BF16)

    def prefix_counts(mask):
        x = mask.astype(F32)
        inc = _dot(x.astype(BF16), upper)
        tot = jnp.broadcast_to(inc[:, w - 1:w], (nb, 128)).astype(BF16)
        before = _dot(strict_lower, tot)
        return inc - x + before[:, 0:1], before

    gt = bits > thr
    eq = bits == thr
    need = capf - total(gt)
    eq_rank, _ = prefix_counts(eq)
    sel = gt | (eq & (eq_rank < need))
    rank, before = prefix_counts(sel)
    pos_ref[0] = jnp.where(sel, rank.astype(jnp.int32) + slot_offset, -1)
    cs_ref[0] = before.astype(jnp.int32) + slot_offset


def _select(aff_blocks, cap, slot_offset):
    n_e, nb, w = aff_blocks.shape
    spec = pl.BlockSpec((1, nb, w), lambda e: (e, 0, 0))
    return pl.pallas_call(
        functools.partial(_select_kernel, cap=cap, slot_offset=slot_offset),
        grid=(n_e,),
        in_specs=[spec],
        out_specs=[spec, pl.BlockSpec((1, nb, 128), lambda e: (e, 0, 0))],
        out_shape=(jax.ShapeDtypeStruct((n_e, nb, w), jnp.int32),
                   jax.ShapeDtypeStruct((n_e, nb, 128), jnp.int32)),
        compiler_params=_params(1),
        name="select",
    )(aff_blocks)


def _dispatch_kernel(c_ref, xa_ref, xb_ref, pos_ref, xd_hbm, stage, onehot_ref, outbuf, sem, nflushed,
                     *, nb_first, nb_tot):
    i = pl.program_id(0)
    R, W = DISPATCH_TILE, COMBINE_WINDOW
    stride = nb_tot + 1
    group = 4

    @pl.when(i == 0)
    def _():
        stage[...] = jnp.zeros_like(stage)
        nflushed[0] = 0

    x = jnp.where(i < nb_first, xa_ref[...], xb_ref[...])
    slot_iota = lax.broadcasted_iota(jnp.int32, (W, R), 0)

    def out_copy(slot, e, row):
        return pltpu.make_async_copy(outbuf.at[slot], xd_hbm.at[e, pl.ds(row, R), :], sem.at[slot])

    def onehot(e, first):
        return jnp.where(pos_ref[0, e:e + 1, :] == slot_iota + first, 1.0, 0.0).astype(BF16)

    for g0 in range(0, N_EXPERTS, group):
        for e in range(g0, g0 + group):
            w0 = (c_ref[e * stride + i] // (W // 2)) * (W // 2)
            onehot_ref[(e - g0) * W:(e - g0 + 1) * W, :] = onehot(e, w0)
        rows = _dot(onehot_ref[...], x).astype(BF16)
        for e in range(g0, g0 + group):
            c0, c1 = c_ref[e * stride + i], c_ref[e * stride + i + 1]
            base = (c0 // R) * R
            w0 = (c0 // (W // 2)) * (W // 2)
            off = pl.multiple_of(w0 - base, W // 2)
            stage[e, pl.ds(off, W), :] += rows[(e - g0) * W:(e - g0 + 1) * W, :]

            def window(w, carry, e=e, base=base, w0=w0):
                first = w0 + w * W
                o2 = pl.multiple_of(first - base, W // 2)
                stage[e, pl.ds(o2, W), :] += _dot(onehot(e, first), x).astype(BF16)
                return carry

            lax.fori_loop(1, jnp.where(c1 > c0, (c1 - w0 + W - 1) // W, 0), window, 0)

            def flush(k, carry, e=e, c0=c0):
                f = nflushed[0]
                slot = f % 2

                @pl.when(f >= 2)
                def _():
                    out_copy(slot, 0, 0).wait()

                outbuf[slot] = stage[e, 0:R, :]
                out_copy(slot, e, pl.multiple_of((c0 // R + k) * R, R)).start()
                stage[e, 0:R, :] = stage[e, R:2 * R, :]
                stage[e, R:2 * R, :] = jnp.zeros((R, D_MODEL), BF16)
                nflushed[0] = f + 1
                return carry

            lax.fori_loop(0, c1 // R - c0 // R, flush, 0)

    @pl.when(i == nb_tot - 1)
    def _():
        out_copy(0, 0, 0).wait()
        out_copy(1, 0, 0).wait()


def _dispatch(cs_ext_flat, x1b_a, x1b_b, pos_blocks, cap_tot):
    nb_tot, n_e, R = pos_blocks.shape
    W = COMBINE_WINDOW
    nb_first = x1b_a.shape[0] // R
    assert n_e * (cap_tot // R) >= 2
    grid_spec = pltpu.PrefetchScalarGridSpec(
        num_scalar_prefetch=1,
        grid=(nb_tot,),
        in_specs=[pl.BlockSpec((R, D_MODEL), lambda i, *_: (jnp.minimum(i, nb_first - 1), 0)),
                  pl.BlockSpec((R, D_MODEL), lambda i, *_: (jnp.maximum(i - nb_first, 0), 0)),
                  pl.BlockSpec((1, n_e, R), lambda i, *_: (i, 0, 0))],
        out_specs=pl.BlockSpec(memory_space=pl.ANY),
        scratch_shapes=[pltpu.VMEM((n_e, 2 * R + W, D_MODEL), BF16),
                        pltpu.VMEM((4 * W, R), BF16),
                        pltpu.VMEM((2, R, D_MODEL), BF16),
                        pltpu.SemaphoreType.DMA((2,)),
                        pltpu.SMEM((1,), jnp.int32)],
    )
    return pl.pallas_call(
        functools.partial(_dispatch_kernel, nb_first=nb_first, nb_tot=nb_tot),
        grid_spec=grid_spec,
        out_shape=jax.ShapeDtypeStruct((n_e, cap_tot, D_MODEL), BF16),
        compiler_params=_params(1),
        name="dispatch",
    )(cs_ext_flat, x1b_a, x1b_b, pos_blocks)


def _expert_kernel(x_ref, wg_ref, wu_ref, wd_ref, y_ref):
    xg = x_ref[0]
    y = jnp.zeros((xg.shape[0], D_MODEL), F32)
    for c in range(EXPERT_FF // FF_CHUNK):
        cs = slice(c * FF_CHUNK, (c + 1) * FF_CHUNK)
        h1 = _dot(xg, wg_ref[0, :, cs])
        h2 = _dot(xg, wu_ref[0, :, cs])
        h = (h1 * jax.nn.sigmoid(h1) * h2).astype(BF16)
        y = y + _dot(h, wd_ref[0, cs, :])
    y_ref[0] = y.astype(y_ref.dtype)


def _experts(xd, wg, wu, wd):
    n_e, cap_tot, _ = xd.shape
    tm = EXPERT_TILE
    assert cap_tot % tm == 0
    rows = pl.BlockSpec((1, tm, D_MODEL), lambda e, j: (e, j, 0))
    return pl.pallas_call(
        _expert_kernel,
        grid=(n_e, cap_tot // tm),
        in_specs=[rows,
                  pl.BlockSpec((1, D_MODEL, EXPERT_FF), lambda e, j: (e, 0, 0)),
                  pl.BlockSpec((1, D_MODEL, EXPERT_FF), lambda e, j: (e, 0, 0)),
                  pl.BlockSpec((1, EXPERT_FF, D_MODEL), lambda e, j: (e, 0, 0))],
        out_specs=rows,
        out_shape=jax.ShapeDtypeStruct((n_e, cap_tot, D_MODEL), BF16),
        compiler_params=_params(2),
        name="experts",
    )(xd, wg, wu, wd)


def _combine_kernel(w0_ref, nwin_ref, x1_ref, pos_ref, aff_ref, y_hbm, g2_ref, b2_ref, o_ref,
                    ymain, yextra, sem_main, sem_extra, gate_ref, acc_ref, *, n_blocks, cap_tot):
    i = pl.program_id(0)
    R, W = DISPATCH_TILE, COMBINE_WINDOW
    half = i % 2

    def main_copy(e, blk, h):
        w0 = pl.multiple_of(w0_ref[e * n_blocks + blk], W // 2)
        return pltpu.make_async_copy(y_hbm.at[e, pl.ds(w0, W), :], ymain.at[h, pl.ds(e * W, W), :],
                                     sem_main.at[h, e])

    def start_main(blk, h):
        for e in range(N_EXPERTS):
            @pl.when(nwin_ref[e * n_blocks + blk] > 0)
            def _(e=e):
                main_copy(e, blk, h).start()

    @pl.when(i == 0)
    def _():
        ymain[...] = jnp.zeros_like(ymain)
        start_main(0, 0)

    @pl.when(i + 1 < n_blocks)
    def _():
        start_main(i + 1, 1 - half)

    slot_iota = lax.broadcasted_iota(jnp.int32, (W, R), 0)

    for e in range(N_EXPERTS):
        slot_ids = slot_iota + w0_ref[e * n_blocks + i]
        gate_ref[e * W:(e + 1) * W, :] = jnp.where(pos_ref[0, e:e + 1, :] == slot_ids,
                                                   aff_ref[0, e:e + 1, :], 0.0).astype(BF16)

    for e in range(N_EXPERTS):
        @pl.when(nwin_ref[e * n_blocks + i] > 0)
        def _(e=e):
            main_copy(e, i, half).wait()

    acc_ref[...] = DEEPNORM_ALPHA * x1_ref[...] + _dot(gate_ref[...], ymain[half], ((0,), (0,)))

    def expert_extras(e, carry):
        w0 = w0_ref[e * n_blocks + i]

        def window(w, c):
            first = w0 + w * W
            ws = pl.multiple_of(jnp.minimum(first, cap_tot - W), W // 2)
            cp = pltpu.make_async_copy(y_hbm.at[e, pl.ds(ws, W), :], yextra, sem_extra)
            cp.start()
            cp.wait()
            slot_ids = slot_iota + ws
            hit = (pos_ref[0, pl.ds(e, 1), :] == slot_ids) & (slot_ids >= first)
            gated = jnp.where(hit, aff_ref[0, pl.ds(e, 1), :], 0.0).astype(BF16)
            acc_ref[...] += _dot(gated, yextra[...], ((0,), (0,)))
            return c

        return lax.fori_loop(1, nwin_ref[e * n_blocks + i], window, carry)

    lax.fori_loop(0, N_EXPERTS, expert_extras, 0)
    o_ref[...] = _layer_norm(acc_ref[...], g2_ref[...], b2_ref[...])


def _combine(w0, nwin, x1, pos_blocks, aff_blocks, y_all, g2, b2, block_offset):
    T = x1.shape[0]
    R, W = DISPATCH_TILE, COMBINE_WINDOW
    n_blocks = T // R
    cap_tot = y_all.shape[1]
    grid_spec = pltpu.PrefetchScalarGridSpec(
        num_scalar_prefetch=2,
        grid=(n_blocks,),
        in_specs=[pl.BlockSpec((R, D_MODEL), lambda i, *_: (i, 0)),
                  pl.BlockSpec((1, N_EXPERTS, R), lambda i, *_: (i + block_offset, 0, 0)),
                  pl.BlockSpec((1, N_EXPERTS, R), lambda i, *_: (i + block_offset, 0, 0)),
                  pl.BlockSpec(memory_space=pl.ANY),
                  pl.BlockSpec((1, D_MODEL), lambda i, *_: (0, 0)),
                  pl.BlockSpec((1, D_MODEL), lambda i, *_: (0, 0))],
        out_specs=pl.BlockSpec((R, D_MODEL), lambda i, *_: (i, 0)),
        scratch_shapes=[pltpu.VMEM((2, N_EXPERTS * W, D_MODEL), BF16),
                        pltpu.VMEM((W, D_MODEL), BF16),
                        pltpu.SemaphoreType.DMA((2, N_EXPERTS)),
                        pltpu.SemaphoreType.DMA(()),
                        pltpu.VMEM((N_EXPERTS * W, R), BF16),
                        pltpu.VMEM((R, D_MODEL), F32)],
    )
    return pl.pallas_call(
        functools.partial(_combine_kernel, n_blocks=n_blocks, cap_tot=cap_tot),
        grid_spec=grid_spec,
        out_shape=jax.ShapeDtypeStruct((T, D_MODEL), F32),
        compiler_params=_params(1),
        name="combine",
    )(w0, nwin, x1, pos_blocks, aff_blocks, y_all, g2, b2)


def _rotary_tables(seq_len):
    half = ROT_DIM // 2
    inv = ROPE_THETA ** (-jnp.arange(half, dtype=F32) / half)
    ang = jnp.arange(seq_len, dtype=F32)[:, None] * inv[None, :]
    cos, sin = jnp.cos(ang), jnp.sin(ang)
    pad = ATT_HEAD_DIM - ROT_DIM
    ones = jnp.ones((seq_len, pad), F32)
    zeros = jnp.zeros((seq_len, pad + half), F32)
    rc = jnp.concatenate([cos, cos, ones], axis=1)
    rsa = jnp.concatenate([-sin, zeros], axis=1)
    rsb = jnp.concatenate([jnp.zeros((seq_len, half), F32), sin, zeros[:, :pad]], axis=1)
    two = lambda t: jnp.concatenate([t, t], axis=1)
    return two(rc), two(rsa), two(rsb)


def _dispatch_plan(cs_all, cap_tot):
    n_e, nb_tot = cs_all.shape
    cs_ext = jnp.concatenate([cs_all, jnp.full((n_e, 1), cap_tot, jnp.int32)], axis=1)
    W = COMBINE_WINDOW
    c0, c1 = cs_ext[:, :-1], cs_ext[:, 1:]
    w0 = jnp.minimum((c0 // (W // 2)) * (W // 2), cap_tot - W)
    nwin = jnp.where(c1 > c0, (c1 - w0 + W - 1) // W, 0)
    return cs_ext.reshape(-1), w0, nwin


def _trunk_front(x, seq_tables, p):
    B, S, D = x.shape
    T = B * S
    x2d = x.reshape(T, D)
    rc, rsa, rsb = seq_tables
    qs, ff, fb, v, gs, qa, kv = _in_proj(x2d, p["g0"], p["b0"], p["w_in"], p["lbf"], p["lbb"],
                                             rc, rsa, rsb, S)
    of = _hgrn(qs, ff, v, B, S, reverse=False)
    ob = _hgrn(qs, fb, v, B, S, reverse=True)
    att = _attn(qa, kv, p["sink"], B, S)
    return _mix(of, ob, gs, att, x2d, p["g0"], p["b0"], p["ng"], p["w_out"], p["g1"], p["b1"], p["w_router_t"])


def kernel(x_prompt, x_sample, emb_ln_g, emb_ln_b, w_in, hgrn_lb_fwd, hgrn_lb_bwd, hgrn_norm_g, attn_sink, w_out, ln1_g, ln1_b, w_router, w_gate, w_up, w_down, ln2_g, ln2_b):
    assert w_in.shape[0] == 1, "single-layer problem"
    R = DISPATCH_TILE
    p = dict(
        g0=emb_ln_g.reshape(1, D_MODEL), b0=emb_ln_b.reshape(1, D_MODEL),
        w_in=w_in[0].astype(BF16), lbf=hgrn_lb_fwd, lbb=hgrn_lb_bwd,
        ng=hgrn_norm_g[0].reshape(1, REC_DK), sink=attn_sink[0],
        w_out=w_out[0].astype(BF16), g1=ln1_g[0].reshape(1, D_MODEL), b1=ln1_b[0].reshape(1, D_MODEL),
        w_router_t=w_router[0].T,
    )
    g2, b2 = ln2_g[0].reshape(1, D_MODEL), ln2_b[0].reshape(1, D_MODEL)
    wg, wu, wd = w_gate[0].astype(BF16), w_up[0].astype(BF16), w_down[0].astype(BF16)

    groups = []
    slot_offset = 0
    for x in (x_prompt, x_sample):
        B, S, _ = x.shape
        T = B * S
        cap = CAPACITY_FACTOR * T // N_EXPERTS
        assert T % TOKEN_TILE == 0 and cap % R == 0
        x1, x1b, aff = _trunk_front(x, _rotary_tables(S), p)
        aff_blocks = aff.reshape(N_EXPERTS, T // R, R)
        pos, cs = _select(aff_blocks, cap, slot_offset)
        groups.append(dict(x=x, x1=x1, x1b=x1b, aff=aff_blocks, pos=pos, cs=cs[:, :, 0]))
        slot_offset += cap
    cap_tot = slot_offset

    pos_all = jnp.concatenate([g["pos"] for g in groups], axis=1)
    aff_all = jnp.concatenate([g["aff"] for g in groups], axis=1)
    cs_all = jnp.concatenate([g["cs"] for g in groups], axis=1)
    cs_ext_flat, w0, nwin = _dispatch_plan(cs_all, cap_tot)
    pos_blocks = jnp.transpose(pos_all, (1, 0, 2))
    aff_blocks = jnp.transpose(aff_all, (1, 0, 2))
    xd = _dispatch(cs_ext_flat, groups[0]["x1b"], groups[1]["x1b"], pos_blocks, cap_tot)
    y_all = _experts(xd, wg, wu, wd)
    outs = []
    block_offset = 0
    for g in groups:
        nb = g["x1"].shape[0] // R
        sl = slice(block_offset, block_offset + nb)
        outs.append(_combine(w0[:, sl].reshape(-1), nwin[:, sl].reshape(-1), g["x1"], pos_blocks, aff_blocks,
                             y_all, g2, b2, block_offset).reshape(g["x"].shape))
        block_offset += nb
    return tuple(outs)
```

```python
import functools

import jax
import jax.numpy as jnp
import numpy as np
from jax import lax
from jax.experimental import pallas as pl
from jax.experimental.pallas import tpu as pltpu

D_MODEL = 1024
REC_WIDTH = 512
ATT_WIDTH = 512
REC_HEADS = 4
REC_DK = 128
REC_CHUNK = 64
ATT_HEADS = 8
ATT_KV_HEADS = 2
ATT_HEAD_DIM = 64
ATT_GROUP = 4
WINDOW = 128
ATT_BLOCK = 128
ROT_DIM = 16
ROPE_THETA = 500000.0
N_EXPERTS = 16
EXPERT_FF = 2048
CAPACITY_FACTOR = 2
NORM_EPS = 1e-5
DEEPNORM_ALPHA = 2.0 ** 0.25
KV_WIDTH = ATT_KV_HEADS * ATT_HEAD_DIM
IN_PROJ_WIDTH = 5 * REC_WIDTH + ATT_WIDTH + 2 * KV_WIDTH

TOKEN_TILE = 512
SEQ_TILE = 512
DISPATCH_TILE = 256
COMBINE_WINDOW = 64
WINDOW_ALIGN = 16
EXPERT_TILE = 512
FF_CHUNK = 512
VMEM_LIMIT_BYTES = 56 * 1024 * 1024

F32 = jnp.float32
BF16 = jnp.bfloat16
HIGHEST = lax.Precision.HIGHEST


def _params(n_axes):
    return pltpu.CompilerParams(dimension_semantics=("arbitrary",) * n_axes,
                                vmem_limit_bytes=VMEM_LIMIT_BYTES)


def _resident(shape):
    nd = len(shape)
    return pl.BlockSpec(shape, lambda *_: (0,) * nd, pipeline_mode=pl.Buffered(1))


def _layer_norm(x, g, b):
    mu = jnp.mean(x, axis=-1, keepdims=True)
    xc = x - mu
    var = jnp.mean(xc * xc, axis=-1, keepdims=True)
    return xc * lax.rsqrt(var + NORM_EPS) * g + b


def _lower_bound(p):
    p0, p1 = p[0:1, :], p[1:2, :]
    m = jnp.maximum(p0, p1)
    e0, e1 = jnp.exp(p0 - m), jnp.exp(p1 - m)
    return e0 / (e0 + e1)


def _dot(a, b, dims=((1,), (0,)), precision=None):
    return lax.dot_general(a, b, (dims, ((), ())), preferred_element_type=F32, precision=precision)


def _in_proj_kernel(x_ref, g_ref, b_ref, w_ref, lbf_ref, lbb_ref, rc_ref, rsa_ref, rsb_ref,
                    qs_ref, ff_ref, fb_ref, v_ref, gs_ref, qa_ref, kv_ref):
    tm = x_ref.shape[0]
    halves = [slice(0, tm // 2), slice(tm // 2, tm)]
    xn = [_layer_norm(x_ref[rows, :], g_ref[...], b_ref[...]).astype(BF16) for rows in halves]
    projs = [_dot(x, w_ref[...]) for x in xn]
    W = REC_WIDTH
    lbf = _lower_bound(lbf_ref[...])
    lbb = _lower_bound(lbb_ref[...])
    for rows, proj in zip(halves, projs):
        q_r, zf, zb = proj[:, 0:W], proj[:, W:2 * W], proj[:, 2 * W:3 * W]
        i_r, g_r = proj[:, 3 * W:4 * W], proj[:, 4 * W:5 * W]
        q_a = proj[:, 5 * W:5 * W + ATT_WIDTH]
        k_a = proj[:, 5 * W + ATT_WIDTH:5 * W + ATT_WIDTH + KV_WIDTH]
        v_a = proj[:, 5 * W + ATT_WIDTH + KV_WIDTH:]
        qs_ref[rows, :] = q_r * jax.nn.sigmoid(q_r)
        ff_ref[rows, :] = lbf + (1.0 - lbf) * jax.nn.sigmoid(zf)
        fb_ref[rows, :] = lbb + (1.0 - lbb) * jax.nn.sigmoid(zb)
        v_ref[rows, :] = i_r.astype(BF16)
        gs_ref[rows, :] = g_r * jax.nn.sigmoid(g_r)
        rc, rsa, rsb = rc_ref[rows, :], rsa_ref[rows, :], rsb_ref[rows, :]
        k_rot = (k_a * rc + pltpu.roll(k_a, KV_WIDTH - ROT_DIM // 2, 1) * rsa
                 + pltpu.roll(k_a, ROT_DIM // 2, 1) * rsb)
        half_kv = KV_WIDTH // 2
        kv_ref[rows, 0:KV_WIDTH] = k_rot.astype(BF16)
        kv_ref[rows, KV_WIDTH:2 * KV_WIDTH] = pltpu.roll(k_rot, half_kv, 1).astype(BF16)
        kv_ref[rows, 2 * KV_WIDTH:3 * KV_WIDTH] = v_a.astype(BF16)
        kv_ref[rows, 3 * KV_WIDTH:] = pltpu.roll(v_a, half_kv, 1).astype(BF16)
        n_rep = ATT_WIDTH // KV_WIDTH
        rc4 = jnp.concatenate([rc] * n_rep, axis=1)
        rsa4 = jnp.concatenate([rsa] * n_rep, axis=1)
        rsb4 = jnp.concatenate([rsb] * n_rep, axis=1)
        q_rot = (q_a * rc4 + pltpu.roll(q_a, ATT_WIDTH - ROT_DIM // 2, 1) * rsa4
                 + pltpu.roll(q_a, ROT_DIM // 2, 1) * rsb4)
        qa_ref[rows, :] = (q_rot * (ATT_HEAD_DIM ** -0.5)).astype(BF16)


def _in_proj(x2d, g, b, w_in_bf16, lbf, lbb, rc, rsa, rsb, seq_len):
    T = x2d.shape[0]
    tm = TOKEN_TILE
    assert T % tm == 0 and seq_len % tm == 0
    n_seq = seq_len // tm
    row = lambda i: (i, 0)
    pos_row = lambda i: (i % n_seq, 0)
    wide = lambda w: pl.BlockSpec((tm, w), row)
    out_shapes = (
        jax.ShapeDtypeStruct((T, REC_WIDTH), F32),
        jax.ShapeDtypeStruct((T, REC_WIDTH), F32),
        jax.ShapeDtypeStruct((T, REC_WIDTH), F32),
        jax.ShapeDtypeStruct((T, REC_WIDTH), BF16),
        jax.ShapeDtypeStruct((T, REC_WIDTH), F32),
        jax.ShapeDtypeStruct((T, ATT_WIDTH), BF16),
        jax.ShapeDtypeStruct((T, 4 * KV_WIDTH), BF16),
    )
    return pl.pallas_call(
        _in_proj_kernel,
        grid=(T // tm,),
        in_specs=[wide(D_MODEL), _resident((1, D_MODEL)), _resident((1, D_MODEL)),
                  _resident((D_MODEL, IN_PROJ_WIDTH)), _resident((2, REC_WIDTH)), _resident((2, REC_WIDTH)),
                  pl.BlockSpec((tm, KV_WIDTH), pos_row), pl.BlockSpec((tm, KV_WIDTH), pos_row),
                  pl.BlockSpec((tm, KV_WIDTH), pos_row)],
        out_specs=[wide(REC_WIDTH)] * 5 + [wide(ATT_WIDTH), wide(4 * KV_WIDTH)],
        out_shape=out_shapes,
        compiler_params=_params(1),
        name="in_proj",
    )(x2d, g, b, w_in_bf16, lbf, lbb, rc, rsa, rsb)


def _hgrn_kernel(q_ref, f_ref, v_ref, o_ref, state_ref, qe_ref, ke_ref, ku_ref, qb_ref, dec_ref,
                 ut_ref, st_ref, b_ref, a_ref, *, reverse):
    C, W = REC_CHUNK, REC_WIDTH
    n_chunks = q_ref.shape[0] // C
    heads = [slice(h * REC_DK, (h + 1) * REC_DK) for h in range(REC_HEADS)]
    chunk_rows = [slice(c * C, (c + 1) * C) for c in range(n_chunks)]

    @pl.when(pl.program_id(1) == 0)
    def _():
        state_ref[...] = jnp.zeros_like(state_ref)

    t_i = lax.broadcasted_iota(jnp.int32, (C, C), 0)
    s_i = lax.broadcasted_iota(jnp.int32, (C, C), 1)
    csum = jnp.where((s_i >= t_i) if reverse else (s_i <= t_i), 1.0, 0.0).astype(BF16)
    mid = C // 2 if reverse else C // 2 - 1
    last = 0 if reverse else C - 1

    for rows in chunk_rows:
        lf = jnp.log(f_ref[rows, :])
        hi = lf.astype(BF16)
        r1 = lf - hi.astype(F32)
        md = r1.astype(BF16)
        lo = (r1 - md.astype(F32)).astype(BF16)
        b3 = _dot(csum, jnp.concatenate([hi, md, lo], axis=1))
        b_ref[rows, :] = b3[:, 0:W] + b3[:, W:2 * W] + b3[:, 2 * W:]
    for c, rows in enumerate(chunk_rows):
        f = f_ref[rows, :]
        q = q_ref[rows, :]
        b = b_ref[rows, :]
        b_mid = b[mid:mid + 1, :]
        b_last = b[last:last + 1, :]
        kk = 1.0 - f
        qe = q * jnp.exp(b - b_mid)
        ke = kk * jnp.exp(b_mid - b)
        qe_ref[rows, :] = qe.astype(BF16)
        ke_ref[rows, :] = ke.astype(BF16)
        qb_ref[rows, :] = (qe * jnp.exp(b_mid)).astype(BF16)
        ku_ref[rows, :] = (ke * jnp.exp(b_last - b_mid)).astype(BF16)
        dec_ref[c:c + 1, :] = jnp.exp(b_last)

    P = 2 * C
    p_t = lax.broadcasted_iota(jnp.int32, (P, P), 0)
    p_s = lax.broadcasted_iota(jnp.int32, (P, P), 1)
    same_chunk = lax.shift_right_logical(p_t, 6) == lax.shift_right_logical(p_s, 6)
    pair_mask = same_chunk & ((p_s >= p_t) if reverse else (p_s <= p_t))
    assert C == 64 and n_chunks % 2 == 0
    pair_rows = [slice(p * P, (p + 1) * P) for p in range(n_chunks // 2)]
    for p, rows in enumerate(pair_rows):
        for h, sl in enumerate(heads):
            a = _dot(qe_ref[rows, sl], ke_ref[rows, sl], ((1,), (1,)))
            a_ref[p, h] = jnp.where(pair_mask, a, 0.0).astype(BF16)
    for p, rows in enumerate(pair_rows):
        for h, sl in enumerate(heads):
            o_ref[rows, sl] = _dot(a_ref[p, h], v_ref[rows, sl])

    for c, rows in enumerate(chunk_rows):
        for h, sl in enumerate(heads):
            ut_ref[h, c] = _dot(v_ref[rows, sl], ku_ref[rows, sl], ((0,), (0,)))

    order = list(reversed(range(n_chunks))) if reverse else list(range(n_chunks))
    for h, sl in enumerate(heads):
        st = state_ref[h]
        for c in order:
            st_ref[h, c] = st.astype(BF16)
            st = st * dec_ref[c:c + 1, sl] + ut_ref[h, c]
        state_ref[h] = st

    for c, rows in enumerate(chunk_rows):
        for h, sl in enumerate(heads):
            o_ref[rows, sl] += _dot(qb_ref[rows, sl], st_ref[h, c], ((1,), (1,)))


def _hgrn(qs, f, v, batch, seq_len, reverse):
    T = qs.shape[0]
    tc = SEQ_TILE
    assert seq_len % tc == 0
    n_s = seq_len // tc
    if reverse:
        idx = lambda b, s: (b * n_s + (n_s - 1 - s), 0)
    else:
        idx = lambda b, s: (b * n_s + s, 0)
    spec = pl.BlockSpec((tc, REC_WIDTH), idx)
    return pl.pallas_call(
        functools.partial(_hgrn_kernel, reverse=reverse),
        grid=(batch, n_s),
        in_specs=[spec, spec, spec],
        out_specs=spec,
        out_shape=jax.ShapeDtypeStruct((T, REC_WIDTH), F32),
        scratch_shapes=[pltpu.VMEM((REC_HEADS, REC_DK, REC_DK), F32)]
        + [pltpu.VMEM((tc, REC_WIDTH), BF16)] * 4
        + [pltpu.VMEM((tc // REC_CHUNK, REC_WIDTH), F32),
           pltpu.VMEM((REC_HEADS, tc // REC_CHUNK, REC_DK, REC_DK), F32),
           pltpu.VMEM((REC_HEADS, tc // REC_CHUNK, REC_DK, REC_DK), BF16),
           pltpu.VMEM((tc, REC_WIDTH), F32),
           pltpu.VMEM((tc // (2 * REC_CHUNK), REC_HEADS, 2 * REC_CHUNK, 2 * REC_CHUNK), BF16)],
        compiler_params=_params(2),
        name="hgrn_bwd" if reverse else "hgrn_fwd",
    )(qs, f, v)


ATT_ROW_CHUNK = 32


def _attn_kernel(sink_ref, q_ref, kvp_ref, kvc_ref, kvn_ref, o_ref, s_ref, p_ref, bias_ref, *, seq_len):
    blk = pl.program_id(1)
    n_q, n_k = ATT_BLOCK, 3 * ATT_BLOCK
    qi = lax.broadcasted_iota(jnp.int32, (n_q, n_k), 0)
    kj = lax.broadcasted_iota(jnp.int32, (n_q, n_k), 1)
    kpos = blk * ATT_BLOCK - ATT_BLOCK + kj
    valid = (jnp.abs(kj - ATT_BLOCK - qi) <= WINDOW) & (kpos >= 0) & (kpos < seq_len)
    bias_ref[...] = jnp.where(valid, 0.0, -1e30)
    band = jnp.concatenate([kvp_ref[...], kvc_ref[...], kvn_ref[...]], axis=0)
    low = lax.broadcasted_iota(jnp.int32, (n_k, KV_WIDTH), 1) < ATT_HEAD_DIM
    zero = jnp.zeros((n_k, KV_WIDTH), BF16)
    k_plain, k_swap = band[:, 0:KV_WIDTH], band[:, KV_WIDTH:2 * KV_WIDTH]
    v_plain, v_swap = band[:, 2 * KV_WIDTH:3 * KV_WIDTH], band[:, 3 * KV_WIDTH:]
    slab = 2 * ATT_HEAD_DIM
    groups = range(ATT_KV_HEADS)
    for g in groups:
        src_lo, src_hi = (k_plain, k_swap) if g == 0 else (k_swap, k_plain)
        k_lo, k_hi = jnp.where(low, src_lo, zero), jnp.where(low, zero, src_hi)
        q2 = jnp.concatenate([q_ref[:, (2 * g) * slab:(2 * g + 1) * slab],
                              q_ref[:, (2 * g + 1) * slab:(2 * g + 2) * slab]], axis=0)
        s_ref[g, 0:2 * n_q, :] = _dot(q2, k_lo, ((1,), (1,)))
        s_ref[g, 2 * n_q:4 * n_q, :] = _dot(q2, k_hi, ((1,), (1,)))
    for g in groups:
        heads = (4 * g, 4 * g + 2, 4 * g + 1, 4 * g + 3)
        for r in range(4 * n_q // ATT_ROW_CHUNK):
            r0 = r * ATT_ROW_CHUNK
            q0 = r0 % n_q
            sink = sink_ref[heads[r0 // n_q]]
            s = s_ref[g, r0:r0 + ATT_ROW_CHUNK, :] + bias_ref[q0:q0 + ATT_ROW_CHUNK, :]
            m = jnp.maximum(jnp.max(s, axis=-1, keepdims=True), sink)
            p = jnp.exp(s - m)
            den = jnp.sum(p, axis=-1, keepdims=True) + jnp.exp(sink - m)
            p_ref[g, r0:r0 + ATT_ROW_CHUNK, :] = (p / den).astype(BF16)
    for g in groups:
        vsrc_lo, vsrc_hi = (v_plain, v_swap) if g == 0 else (v_swap, v_plain)
        v_lo, v_hi = jnp.where(low, vsrc_lo, zero), jnp.where(low, zero, vsrc_hi)
        for i in range(2):
            lo_rows = p_ref[g, i * n_q:(i + 1) * n_q, :]
            hi_rows = p_ref[g, (2 + i) * n_q:(3 + i) * n_q, :]
            o = _dot(lo_rows, v_lo) + _dot(hi_rows, v_hi)
            o_ref[:, (2 * g + i) * slab:(2 * g + i + 1) * slab] = o.astype(o_ref.dtype)


def _attn(qa, kv, sink, batch, seq_len):
    T = qa.shape[0]
    n_b = seq_len // ATT_BLOCK
    cur = lambda b, i: (b * n_b + i, 0)
    prev = lambda b, i: (b * n_b + jnp.maximum(i - 1, 0), 0)
    nxt = lambda b, i: (b * n_b + jnp.minimum(i + 1, n_b - 1), 0)
    kvs = lambda im: pl.BlockSpec((ATT_BLOCK, 4 * KV_WIDTH), im)
    n_rows = ATT_GROUP * ATT_BLOCK
    return pl.pallas_call(
        functools.partial(_attn_kernel, seq_len=seq_len),
        grid=(batch, n_b),
        in_specs=[pl.BlockSpec(memory_space=pltpu.SMEM),
                  pl.BlockSpec((ATT_BLOCK, ATT_WIDTH), cur), kvs(prev), kvs(cur), kvs(nxt)],
        out_specs=pl.BlockSpec((ATT_BLOCK, ATT_WIDTH), cur),
        out_shape=jax.ShapeDtypeStruct((T, ATT_WIDTH), BF16),
        scratch_shapes=[pltpu.VMEM((ATT_KV_HEADS, n_rows, 3 * ATT_BLOCK), F32),
                        pltpu.VMEM((ATT_KV_HEADS, n_rows, 3 * ATT_BLOCK), BF16),
                        pltpu.VMEM((ATT_BLOCK, 3 * ATT_BLOCK), F32)],
        compiler_params=_params(2),
        name="attn",
    )(sink, qa, kv, kv, kv)


def _mix_kernel(of_ref, ob_ref, gs_ref, att_ref, x_ref, g0_ref, b0_ref, ng_ref, wo_ref,
                g1_ref, b1_ref, wr_ref, x1_ref, x1b_ref, aff_ref):
    tm = x_ref.shape[0]
    halves = [slice(0, tm // 2), slice(tm // 2, tm)]
    ng = ng_ref[...]

    def gated_rec(rows):
        o_rec = of_ref[rows, :] + ob_ref[rows, :]
        gs = gs_ref[rows, :]
        recs = []
        for h in range(REC_HEADS):
            sl = slice(h * REC_DK, (h + 1) * REC_DK)
            oh = o_rec[:, sl]
            ms = jnp.mean(oh * oh, axis=-1, keepdims=True)
            recs.append(oh * lax.rsqrt(ms + NORM_EPS) * ng * gs[:, sl])
        return jnp.concatenate(recs, axis=1).astype(BF16)

    recs = [gated_rec(rows) for rows in halves]
    mixes = [_dot(rec, wo_ref[0:REC_WIDTH, :]) + _dot(att_ref[rows, :], wo_ref[REC_WIDTH:, :])
             for rows, rec in zip(halves, recs)]
    x1s = []
    for rows, mix in zip(halves, mixes):
        x0 = _layer_norm(x_ref[rows, :], g0_ref[...], b0_ref[...])
        x1 = _layer_norm(DEEPNORM_ALPHA * x0 + mix, g1_ref[...], b1_ref[...])
        x1_ref[rows, :] = x1
        x1b_ref[rows, :] = x1.astype(BF16)
        x1s.append(x1)
    logits = [_dot(wr_ref[...], x1, ((1,), (1,)), precision=HIGHEST) for x1 in x1s]
    for rows, lg in zip(halves, logits):
        m = jnp.max(lg, axis=0, keepdims=True)
        e = jnp.exp(lg - m)
        aff_ref[:, rows] = e / jnp.sum(e, axis=0, keepdims=True)


def _mix(of, ob, gs, att, x2d, g0, b0, ng, w_out_bf16, g1, b1, w_router_t):
    T = x2d.shape[0]
    tm = TOKEN_TILE
    row = lambda i: (i, 0)
    wide = lambda w: pl.BlockSpec((tm, w), row)
    vec = _resident((1, D_MODEL))
    return pl.pallas_call(
        _mix_kernel,
        grid=(T // tm,),
        in_specs=[wide(REC_WIDTH), wide(REC_WIDTH), wide(REC_WIDTH), wide(ATT_WIDTH), wide(D_MODEL),
                  vec, vec, _resident((1, REC_DK)), _resident((D_MODEL, D_MODEL)), vec, vec,
                  _resident((N_EXPERTS, D_MODEL))],
        out_specs=[wide(D_MODEL), wide(D_MODEL), pl.BlockSpec((N_EXPERTS, tm), lambda i: (0, i))],
        out_shape=(jax.ShapeDtypeStruct((T, D_MODEL), F32),
                   jax.ShapeDtypeStruct((T, D_MODEL), BF16),
                   jax.ShapeDtypeStruct((N_EXPERTS, T), F32)),
        compiler_params=_params(1),
        name="mix",
    )(of, ob, gs, att, x2d, g0, b0, ng, w_out_bf16, g1, b1, w_router_t)


def _select_kernel(aff_ref, pos_ref, cs_ref, *, cap, slot_offset):
    aff = aff_ref[0]
    nb, w = aff.shape
    bits = pltpu.bitcast(aff, jnp.int32)
    capf = jnp.float32(cap)

    def total(mask):
        x = mask.astype(F32)
        return jnp.sum(jnp.sum(x, axis=0, keepdims=True), axis=1, keepdims=True)

    def search(i, prefix):
        cand = prefix | lax.shift_left(jnp.int32(1), 30 - i)
        return jnp.where(total(bits >= cand) >= capf, cand, prefix)

    thr = lax.fori_loop(0, 31, search, jnp.zeros((1, 1), jnp.int32))

    c_i = lax.broadcasted_iota(jnp.int32, (w, w), 0)
    c_j = lax.broadcasted_iota(jnp.int32, (w, w), 1)
    upper = (c_i <= c_j).astype(BF16)
    r_i = lax.broadcasted_iota(jnp.int32, (nb, nb), 0)
    r_j = lax.broadcasted_iota(jnp.int32, (nb, nb), 1)
    strict_lower = (r_j < r_i).astype(BF16)

    def prefix_counts(mask):
        x = mask.astype(F32)
        inc = _dot(x.astype(BF16), upper)
        tot = jnp.broadcast_to(inc[:, w - 1:w], (nb, 128)).astype(BF16)
        before = _dot(strict_lower, tot)
        return inc - x + before[:, 0:1], before

    gt = bits > thr
    eq = bits == thr
    need = capf - total(gt)
    eq_rank, _ = prefix_counts(eq)
    sel = gt | (eq & (eq_rank < need))
    rank, before = prefix_counts(sel)
    pos_ref[0] = jnp.where(sel, rank.astype(jnp.int32) + slot_offset, -1)
    cs_ref[0] = before.astype(jnp.int32) + slot_offset


def _select(aff_blocks, cap, slot_offset):
    n_e, nb, w = aff_blocks.shape
    spec = pl.BlockSpec((1, nb, w), lambda e: (e, 0, 0))
    return pl.pallas_call(
        functools.partial(_select_kernel, cap=cap, slot_offset=slot_offset),
        grid=(n_e,),
        in_specs=[spec],
        out_specs=[spec, pl.BlockSpec((1, nb, 128), lambda e: (e, 0, 0))],
        out_shape=(jax.ShapeDtypeStruct((n_e, nb, w), jnp.int32),
                   jax.ShapeDtypeStruct((n_e, nb, 128), jnp.int32)),
        compiler_params=_params(1),
        name="select",
    )(aff_blocks)


def _dispatch_kernel(c_ref, xa_ref, xb_ref, pos_ref, xd_hbm, stage, onehot_ref, outbuf, sem, nflushed,
                     *, nb_first, nb_tot):
    i = pl.program_id(0)
    R, W = DISPATCH_TILE, COMBINE_WINDOW
    stride = nb_tot + 1
    group = 4

    @pl.when(i == 0)
    def _():
        stage[...] = jnp.zeros_like(stage)
        nflushed[0] = 0

    x = jnp.where(i < nb_first, xa_ref[...], xb_ref[...])
    slot_iota = lax.broadcasted_iota(jnp.int32, (W, R), 0)

    def out_copy(slot, e, row):
        return pltpu.make_async_copy(outbuf.at[slot], xd_hbm.at[e, pl.ds(row, R), :], sem.at[slot])

    def onehot(e, first):
        return jnp.where(pos_ref[0, e:e + 1, :] == slot_iota + first, 1.0, 0.0).astype(BF16)

    experts = range(N_EXPERTS)
    c0s = [c_ref[e * stride + i] for e in experts]
    c1s = [c_ref[e * stride + i + 1] for e in experts]
    bases = [(c0 // R) * R for c0 in c0s]
    w0s = [(c0 // WINDOW_ALIGN) * WINDOW_ALIGN for c0 in c0s]

    for e in experts:
        onehot_ref[e * W:(e + 1) * W, :] = onehot(e, w0s[e])
    moved = [_dot(onehot_ref[g0 * W:(g0 + group) * W, :], x).astype(BF16)
             for g0 in range(0, N_EXPERTS, group)]
    for e in experts:
        off = pl.multiple_of(w0s[e] - bases[e], WINDOW_ALIGN)
        r0 = (e % group) * W
        stage[e, pl.ds(off, W), :] += moved[e // group][r0:r0 + W, :]

    for e in experts:
        def window(w, carry, e=e):
            first = w0s[e] + w * W
            o2 = pl.multiple_of(first - bases[e], WINDOW_ALIGN)
            stage[e, pl.ds(o2, W), :] += _dot(onehot(e, first), x).astype(BF16)
            return carry

        lax.fori_loop(1, jnp.where(c1s[e] > c0s[e], (c1s[e] - w0s[e] + W - 1) // W, 0), window, 0)

    for e in experts:
        def flush(k, carry, e=e):
            f = nflushed[0]
            slot = f % 2

            @pl.when(f >= 2)
            def _():
                out_copy(slot, 0, 0).wait()

            outbuf[slot] = stage[e, 0:R, :]
            out_copy(slot, e, pl.multiple_of((c0s[e] // R + k) * R, R)).start()
            stage[e, 0:R, :] = stage[e, R:2 * R, :]
            stage[e, R:2 * R, :] = jnp.zeros((R, D_MODEL), BF16)
            nflushed[0] = f + 1
            return carry

        lax.fori_loop(0, c1s[e] // R - c0s[e] // R, flush, 0)

    @pl.when(i == nb_tot - 1)
    def _():
        out_copy(0, 0, 0).wait()
        out_copy(1, 0, 0).wait()


def _dispatch(cs_ext_flat, x1b_a, x1b_b, pos_blocks, cap_tot):
    nb_tot, n_e, R = pos_blocks.shape
    W = COMBINE_WINDOW
    nb_first = x1b_a.shape[0] // R
    assert n_e * (cap_tot // R) >= 2
    grid_spec = pltpu.PrefetchScalarGridSpec(
        num_scalar_prefetch=1,
        grid=(nb_tot,),
        in_specs=[pl.BlockSpec((R, D_MODEL), lambda i, *_: (jnp.minimum(i, nb_first - 1), 0)),
                  pl.BlockSpec((R, D_MODEL), lambda i, *_: (jnp.maximum(i - nb_first, 0), 0)),
                  pl.BlockSpec((1, n_e, R), lambda i, *_: (i, 0, 0))],
        out_specs=pl.BlockSpec(memory_space=pl.ANY),
        scratch_shapes=[pltpu.VMEM((n_e, 2 * R + W, D_MODEL), BF16),
                        pltpu.VMEM((n_e * W, R), BF16),
                        pltpu.VMEM((2, R, D_MODEL), BF16),
                        pltpu.SemaphoreType.DMA((2,)),
                        pltpu.SMEM((1,), jnp.int32)],
    )
    return pl.pallas_call(
        functools.partial(_dispatch_kernel, nb_first=nb_first, nb_tot=nb_tot),
        grid_spec=grid_spec,
        out_shape=jax.ShapeDtypeStruct((n_e, cap_tot, D_MODEL), BF16),
        compiler_params=_params(1),
        name="dispatch",
    )(cs_ext_flat, x1b_a, x1b_b, pos_blocks)


def _expert_kernel(x_ref, wg_ref, wu_ref, wd_ref, y_ref):
    xg = x_ref[0]
    y = jnp.zeros((xg.shape[0], D_MODEL), F32)
    for c in range(EXPERT_FF // FF_CHUNK):
        cs = slice(c * FF_CHUNK, (c + 1) * FF_CHUNK)
        h1 = _dot(xg, wg_ref[0, :, cs])
        h2 = _dot(xg, wu_ref[0, :, cs])
        h = (h1 * jax.nn.sigmoid(h1) * h2).astype(BF16)
        y = y + _dot(h, wd_ref[0, cs, :])
    y_ref[0] = y.astype(y_ref.dtype)


def _experts(xd, wg, wu, wd):
    n_e, cap_tot, _ = xd.shape
    tm = EXPERT_TILE
    assert cap_tot % tm == 0
    rows = pl.BlockSpec((1, tm, D_MODEL), lambda e, j: (e, j, 0))
    return pl.pallas_call(
        _expert_kernel,
        grid=(n_e, cap_tot // tm),
        in_specs=[rows,
                  pl.BlockSpec((1, D_MODEL, EXPERT_FF), lambda e, j: (e, 0, 0)),
                  pl.BlockSpec((1, D_MODEL, EXPERT_FF), lambda e, j: (e, 0, 0)),
                  pl.BlockSpec((1, EXPERT_FF, D_MODEL), lambda e, j: (e, 0, 0))],
        out_specs=rows,
        out_shape=jax.ShapeDtypeStruct((n_e, cap_tot, D_MODEL), BF16),
        compiler_params=_params(2),
        name="experts",
    )(xd, wg, wu, wd)


def _combine_kernel(w0_ref, nwin_ref, x1_ref, pos_ref, aff_ref, y_hbm, g2_ref, b2_ref, o_ref,
                    ymain, yextra, sem_main, sem_extra, gate_ref, acc_ref, *, n_blocks, cap_tot):
    i = pl.program_id(0)
    R, W = DISPATCH_TILE, COMBINE_WINDOW
    half = i % 2

    def main_copy(e, blk, h):
        w0 = pl.multiple_of(w0_ref[e * n_blocks + blk], WINDOW_ALIGN)
        return pltpu.make_async_copy(y_hbm.at[e, pl.ds(w0, W), :], ymain.at[h, pl.ds(e * W, W), :],
                                     sem_main.at[h, e])

    def start_main(blk, h):
        for e in range(N_EXPERTS):
            @pl.when(nwin_ref[e * n_blocks + blk] > 0)
            def _(e=e):
                main_copy(e, blk, h).start()

    @pl.when(i == 0)
    def _():
        ymain[...] = jnp.zeros_like(ymain)
        start_main(0, 0)

    @pl.when(i + 1 < n_blocks)
    def _():
        start_main(i + 1, 1 - half)

    slot_iota = lax.broadcasted_iota(jnp.int32, (W, R), 0)

    for e in range(N_EXPERTS):
        slot_ids = slot_iota + w0_ref[e * n_blocks + i]
        gate_ref[e * W:(e + 1) * W, :] = jnp.where(pos_ref[0, e:e + 1, :] == slot_ids,
                                                   aff_ref[0, e:e + 1, :], 0.0).astype(BF16)

    for e in range(N_EXPERTS):
        @pl.when(nwin_ref[e * n_blocks + i] > 0)
        def _(e=e):
            main_copy(e, i, half).wait()

    acc_ref[...] = DEEPNORM_ALPHA * x1_ref[...] + _dot(gate_ref[...], ymain[half], ((0,), (0,)))

    def expert_extras(e, carry):
        w0 = w0_ref[e * n_blocks + i]

        def window(w, c):
            first = w0 + w * W
            ws = pl.multiple_of(jnp.minimum(first, cap_tot - W), WINDOW_ALIGN)
            cp = pltpu.make_async_copy(y_hbm.at[e, pl.ds(ws, W), :], yextra, sem_extra)
            cp.start()
            cp.wait()
            slot_ids = slot_iota + ws
            hit = (pos_ref[0, pl.ds(e, 1), :] == slot_ids) & (slot_ids >= first)
            gated = jnp.where(hit, aff_ref[0, pl.ds(e, 1), :], 0.0).astype(BF16)
            acc_ref[...] += _dot(gated, yextra[...], ((0,), (0,)))
            return c

        return lax.fori_loop(1, nwin_ref[e * n_blocks + i], window, carry)

    lax.fori_loop(0, N_EXPERTS, expert_extras, 0)
    o_ref[...] = _layer_norm(acc_ref[...], g2_ref[...], b2_ref[...])


def _combine(w0, nwin, x1, pos_blocks, aff_blocks, y_all, g2, b2, block_offset):
    T = x1.shape[0]
    R, W = DISPATCH_TILE, COMBINE_WINDOW
    n_blocks = T // R
    cap_tot = y_all.shape[1]
    grid_spec = pltpu.PrefetchScalarGridSpec(
        num_scalar_prefetch=2,
        grid=(n_blocks,),
        in_specs=[pl.BlockSpec((R, D_MODEL), lambda i, *_: (i, 0)),
                  pl.BlockSpec((1, N_EXPERTS, R), lambda i, *_: (i + block_offset, 0, 0)),
                  pl.BlockSpec((1, N_EXPERTS, R), lambda i, *_: (i + block_offset, 0, 0)),
                  pl.BlockSpec(memory_space=pl.ANY),
                  pl.BlockSpec((1, D_MODEL), lambda i, *_: (0, 0)),
                  pl.BlockSpec((1, D_MODEL), lambda i, *_: (0, 0))],
        out_specs=pl.BlockSpec((R, D_MODEL), lambda i, *_: (i, 0)),
        scratch_shapes=[pltpu.VMEM((2, N_EXPERTS * W, D_MODEL), BF16),
                        pltpu.VMEM((W, D_MODEL), BF16),
                        pltpu.SemaphoreType.DMA((2, N_EXPERTS)),
                        pltpu.SemaphoreType.DMA(()),
                        pltpu.VMEM((N_EXPERTS * W, R), BF16),
                        pltpu.VMEM((R, D_MODEL), F32)],
    )
    return pl.pallas_call(
        functools.partial(_combine_kernel, n_blocks=n_blocks, cap_tot=cap_tot),
        grid_spec=grid_spec,
        out_shape=jax.ShapeDtypeStruct((T, D_MODEL), F32),
        compiler_params=_params(1),
        name="combine",
    )(w0, nwin, x1, pos_blocks, aff_blocks, y_all, g2, b2)


def _rotary_tables(seq_len):
    half = ROT_DIM // 2
    inv = ROPE_THETA ** (-jnp.arange(half, dtype=F32) / half)
    ang = jnp.arange(seq_len, dtype=F32)[:, None] * inv[None, :]
    cos, sin = jnp.cos(ang), jnp.sin(ang)
    pad = ATT_HEAD_DIM - ROT_DIM
    ones = jnp.ones((seq_len, pad), F32)
    zeros = jnp.zeros((seq_len, pad + half), F32)
    rc = jnp.concatenate([cos, cos, ones], axis=1)
    rsa = jnp.concatenate([-sin, zeros], axis=1)
    rsb = jnp.concatenate([jnp.zeros((seq_len, half), F32), sin, zeros[:, :pad]], axis=1)
    two = lambda t: jnp.concatenate([t, t], axis=1)
    return two(rc), two(rsa), two(rsb)


def _dispatch_plan(cs_all, cap_tot):
    n_e, nb_tot = cs_all.shape
    cs_ext = jnp.concatenate([cs_all, jnp.full((n_e, 1), cap_tot, jnp.int32)], axis=1)
    W = COMBINE_WINDOW
    c0, c1 = cs_ext[:, :-1], cs_ext[:, 1:]
    w0 = jnp.minimum((c0 // WINDOW_ALIGN) * WINDOW_ALIGN, cap_tot - W)
    nwin = jnp.where(c1 > c0, (c1 - w0 + W - 1) // W, 0)
    return cs_ext.reshape(-1), w0, nwin


def _trunk_front(x, seq_tables, p):
    B, S, D = x.shape
    T = B * S
    x2d = x.reshape(T, D)
    rc, rsa, rsb = seq_tables
    qs, ff, fb, v, gs, qa, kv = _in_proj(x2d, p["g0"], p["b0"], p["w_in"], p["lbf"], p["lbb"],
                                             rc, rsa, rsb, S)
    of = _hgrn(qs, ff, v, B, S, reverse=False)
    ob = _hgrn(qs, fb, v, B, S, reverse=True)
    att = _attn(qa, kv, p["sink"], B, S)
    return _mix(of, ob, gs, att, x2d, p["g0"], p["b0"], p["ng"], p["w_out"], p["g1"], p["b1"], p["w_router_t"])


def kernel(x_prompt, x_sample, emb_ln_g, emb_ln_b, w_in, hgrn_lb_fwd, hgrn_lb_bwd, hgrn_norm_g, attn_sink, w_out, ln1_g, ln1_b, w_router, w_gate, w_up, w_down, ln2_g, ln2_b):
    assert w_in.shape[0] == 1, "single-layer problem"
    R = DISPATCH_TILE
    p = dict(
        g0=emb_ln_g.reshape(1, D_MODEL), b0=emb_ln_b.reshape(1, D_MODEL),
        w_in=w_in[0].astype(BF16), lbf=hgrn_lb_fwd, lbb=hgrn_lb_bwd,
        ng=hgrn_norm_g[0].reshape(1, REC_DK), sink=attn_sink[0],
        w_out=w_out[0].astype(BF16), g1=ln1_g[0].reshape(1, D_MODEL), b1=ln1_b[0].reshape(1, D_MODEL),
        w_router_t=w_router[0].T,
    )
    g2, b2 = ln2_g[0].reshape(1, D_MODEL), ln2_b[0].reshape(1, D_MODEL)
    wg, wu, wd = w_gate[0].astype(BF16), w_up[0].astype(BF16), w_down[0].astype(BF16)

    groups = []
    slot_offset = 0
    for x in (x_prompt, x_sample):
        B, S, _ = x.shape
        T = B * S
        cap = CAPACITY_FACTOR * T // N_EXPERTS
        assert T % TOKEN_TILE == 0 and cap % R == 0
        x1, x1b, aff = _trunk_front(x, _rotary_tables(S), p)
        aff_blocks = aff.reshape(N_EXPERTS, T // R, R)
        pos, cs = _select(aff_blocks, cap, slot_offset)
        groups.append(dict(x=x, x1=x1, x1b=x1b, aff=aff_blocks, pos=pos, cs=cs[:, :, 0]))
        slot_offset += cap
    cap_tot = slot_offset

    pos_all = jnp.concatenate([g["pos"] for g in groups], axis=1)
    aff_all = jnp.concatenate([g["aff"] for g in groups], axis=1)
    cs_all = jnp.concatenate([g["cs"] for g in groups], axis=1)
    cs_ext_flat, w0, nwin = _dispatch_plan(cs_all, cap_tot)
    pos_blocks = jnp.transpose(pos_all, (1, 0, 2))
    aff_blocks = jnp.transpose(aff_all, (1, 0, 2))
    xd = _dispatch(cs_ext_flat, groups[0]["x1b"], groups[1]["x1b"], pos_blocks, cap_tot)
    y_all = _experts(xd, wg, wu, wd)
    outs = []
    block_offset = 0
    for g in groups:
        nb = g["x1"].shape[0] // R
        sl = slice(block_offset, block_offset + nb)
        outs.append(_combine(w0[:, sl].reshape(-1), nwin[:, sl].reshape(-1), g["x1"], pos_blocks, aff_blocks,
                             y_all, g2, b2, block_offset).reshape(g["x"].shape))
        block_offset += nb
    return tuple(outs)
```

```python
import functools

import jax
import jax.numpy as jnp
import numpy as np
from jax import lax
from jax.experimental import pallas as pl
from jax.experimental.pallas import tpu as pltpu

D_MODEL = 1024
REC_WIDTH = 512
ATT_WIDTH = 512
REC_HEADS = 4
REC_DK = 128
REC_CHUNK = 64
ATT_HEADS = 8
ATT_KV_HEADS = 2
ATT_HEAD_DIM = 64
ATT_GROUP = 4
WINDOW = 128
ATT_BLOCK = 128
ROT_DIM = 16
ROPE_THETA = 500000.0
N_EXPERTS = 16
EXPERT_FF = 2048
CAPACITY_FACTOR = 2
NORM_EPS = 1e-5
DEEPNORM_ALPHA = 2.0 ** 0.25
KV_WIDTH = ATT_KV_HEADS * ATT_HEAD_DIM
IN_PROJ_WIDTH = 5 * REC_WIDTH + ATT_WIDTH + 2 * KV_WIDTH

TOKEN_TILE = 512
SEQ_TILE = 512
DISPATCH_TILE = 256
COMBINE_WINDOW = 64
WINDOW_ALIGN = 16
EXPERT_TILE = 512
FF_CHUNK = 512
VMEM_LIMIT_BYTES = 56 * 1024 * 1024

F32 = jnp.float32
BF16 = jnp.bfloat16
HIGHEST = lax.Precision.HIGHEST


def _params(n_axes):
    return pltpu.CompilerParams(dimension_semantics=("arbitrary",) * n_axes,
                                vmem_limit_bytes=VMEM_LIMIT_BYTES)


def _resident(shape):
    nd = len(shape)
    return pl.BlockSpec(shape, lambda *_: (0,) * nd, pipeline_mode=pl.Buffered(1))


def _layer_norm(x, g, b):
    mu = jnp.mean(x, axis=-1, keepdims=True)
    xc = x - mu
    var = jnp.mean(xc * xc, axis=-1, keepdims=True)
    return xc * lax.rsqrt(var + NORM_EPS) * g + b


def _lower_bound(p):
    p0, p1 = p[0:1, :], p[1:2, :]
    m = jnp.maximum(p0, p1)
    e0, e1 = jnp.exp(p0 - m), jnp.exp(p1 - m)
    return e0 / (e0 + e1)


def _dot(a, b, dims=((1,), (0,)), precision=None):
    return lax.dot_general(a, b, (dims, ((), ())), preferred_element_type=F32, precision=precision)


def _in_proj_kernel(x_ref, g_ref, b_ref, w_ref, lbf_ref, lbb_ref, rc_ref, rsa_ref, rsb_ref,
                    qs_ref, ff_ref, fb_ref, v_ref, gs_ref, qa_ref, kv_ref, x0_ref):
    tm = x_ref.shape[0]
    halves = [slice(0, tm // 2), slice(tm // 2, tm)]
    xn = []
    for rows in halves:
        x0 = _layer_norm(x_ref[rows, :], g_ref[...], b_ref[...])
        x0_ref[rows, :] = x0
        xn.append(x0.astype(BF16))
    projs = [_dot(x, w_ref[...]) for x in xn]
    W = REC_WIDTH
    lbf = _lower_bound(lbf_ref[...])
    lbb = _lower_bound(lbb_ref[...])
    for rows, proj in zip(halves, projs):
        q_r, zf, zb = proj[:, 0:W], proj[:, W:2 * W], proj[:, 2 * W:3 * W]
        i_r, g_r = proj[:, 3 * W:4 * W], proj[:, 4 * W:5 * W]
        q_a = proj[:, 5 * W:5 * W + ATT_WIDTH]
        k_a = proj[:, 5 * W + ATT_WIDTH:5 * W + ATT_WIDTH + KV_WIDTH]
        v_a = proj[:, 5 * W + ATT_WIDTH + KV_WIDTH:]
        qs_ref[rows, :] = q_r * jax.nn.sigmoid(q_r)
        ff_ref[rows, :] = lbf + (1.0 - lbf) * jax.nn.sigmoid(zf)
        fb_ref[rows, :] = lbb + (1.0 - lbb) * jax.nn.sigmoid(zb)
        v_ref[rows, :] = i_r.astype(BF16)
        gs_ref[rows, :] = g_r * jax.nn.sigmoid(g_r)
        rc, rsa, rsb = rc_ref[rows, :], rsa_ref[rows, :], rsb_ref[rows, :]
        k_rot = (k_a * rc + pltpu.roll(k_a, KV_WIDTH - ROT_DIM // 2, 1) * rsa
                 + pltpu.roll(k_a, ROT_DIM // 2, 1) * rsb)
        half_kv = KV_WIDTH // 2
        kv_ref[rows, 0:KV_WIDTH] = k_rot.astype(BF16)
        kv_ref[rows, KV_WIDTH:2 * KV_WIDTH] = pltpu.roll(k_rot, half_kv, 1).astype(BF16)
        kv_ref[rows, 2 * KV_WIDTH:3 * KV_WIDTH] = v_a.astype(BF16)
        kv_ref[rows, 3 * KV_WIDTH:] = pltpu.roll(v_a, half_kv, 1).astype(BF16)
        n_rep = ATT_WIDTH // KV_WIDTH
        rc4 = jnp.concatenate([rc] * n_rep, axis=1)
        rsa4 = jnp.concatenate([rsa] * n_rep, axis=1)
        rsb4 = jnp.concatenate([rsb] * n_rep, axis=1)
        q_rot = (q_a * rc4 + pltpu.roll(q_a, ATT_WIDTH - ROT_DIM // 2, 1) * rsa4
                 + pltpu.roll(q_a, ROT_DIM // 2, 1) * rsb4)
        qa_ref[rows, :] = (q_rot * (ATT_HEAD_DIM ** -0.5)).astype(BF16)


def _in_proj(x2d, g, b, w_in_bf16, lbf, lbb, rc, rsa, rsb, seq_len):
    T = x2d.shape[0]
    tm = TOKEN_TILE
    assert T % tm == 0 and seq_len % tm == 0
    n_seq = seq_len // tm
    row = lambda i: (i, 0)
    pos_row = lambda i: (i % n_seq, 0)
    wide = lambda w: pl.BlockSpec((tm, w), row)
    out_shapes = (
        jax.ShapeDtypeStruct((T, REC_WIDTH), F32),
        jax.ShapeDtypeStruct((T, REC_WIDTH), F32),
        jax.ShapeDtypeStruct((T, REC_WIDTH), F32),
        jax.ShapeDtypeStruct((T, REC_WIDTH), BF16),
        jax.ShapeDtypeStruct((T, REC_WIDTH), F32),
        jax.ShapeDtypeStruct((T, ATT_WIDTH), BF16),
        jax.ShapeDtypeStruct((T, 4 * KV_WIDTH), BF16),
        jax.ShapeDtypeStruct((T, D_MODEL), F32),
    )
    return pl.pallas_call(
        _in_proj_kernel,
        grid=(T // tm,),
        in_specs=[wide(D_MODEL), _resident((1, D_MODEL)), _resident((1, D_MODEL)),
                  _resident((D_MODEL, IN_PROJ_WIDTH)), _resident((2, REC_WIDTH)), _resident((2, REC_WIDTH)),
                  pl.BlockSpec((tm, KV_WIDTH), pos_row), pl.BlockSpec((tm, KV_WIDTH), pos_row),
                  pl.BlockSpec((tm, KV_WIDTH), pos_row)],
        out_specs=[wide(REC_WIDTH)] * 5 + [wide(ATT_WIDTH), wide(4 * KV_WIDTH), wide(D_MODEL)],
        out_shape=out_shapes,
        compiler_params=_params(1),
        name="in_proj",
    )(x2d, g, b, w_in_bf16, lbf, lbb, rc, rsa, rsb)


def _hgrn_kernel(q_ref, f_ref, v_ref, o_ref, state_ref, qe_ref, ke_ref, ku_ref, qb_ref, dec_ref,
                 ut_ref, st_ref, b_ref, a_ref, *, reverse):
    C, W = REC_CHUNK, REC_WIDTH
    n_chunks = q_ref.shape[0] // C
    heads = [slice(h * REC_DK, (h + 1) * REC_DK) for h in range(REC_HEADS)]
    chunk_rows = [slice(c * C, (c + 1) * C) for c in range(n_chunks)]

    @pl.when(pl.program_id(1) == 0)
    def _():
        state_ref[...] = jnp.zeros_like(state_ref)

    t_i = lax.broadcasted_iota(jnp.int32, (C, C), 0)
    s_i = lax.broadcasted_iota(jnp.int32, (C, C), 1)
    csum = jnp.where((s_i >= t_i) if reverse else (s_i <= t_i), 1.0, 0.0).astype(BF16)
    mid = C // 2 if reverse else C // 2 - 1
    last = 0 if reverse else C - 1

    for rows in chunk_rows:
        lf = jnp.log(f_ref[rows, :])
        hi = lf.astype(BF16)
        r1 = lf - hi.astype(F32)
        md = r1.astype(BF16)
        lo = (r1 - md.astype(F32)).astype(BF16)
        b3 = _dot(csum, jnp.concatenate([hi, md, lo], axis=1))
        b_ref[rows, :] = b3[:, 0:W] + b3[:, W:2 * W] + b3[:, 2 * W:]
    for c, rows in enumerate(chunk_rows):
        f = f_ref[rows, :]
        q = q_ref[rows, :]
        b = b_ref[rows, :]
        b_mid = b[mid:mid + 1, :]
        b_last = b[last:last + 1, :]
        kk = 1.0 - f
        qe = q * jnp.exp(b - b_mid)
        ke = kk * jnp.exp(b_mid - b)
        qe_ref[rows, :] = qe.astype(BF16)
        ke_ref[rows, :] = ke.astype(BF16)
        qb_ref[rows, :] = (qe * jnp.exp(b_mid)).astype(BF16)
        ku_ref[rows, :] = (ke * jnp.exp(b_last - b_mid)).astype(BF16)
        dec_ref[c:c + 1, :] = jnp.exp(b_last)

    P = 2 * C
    p_t = lax.broadcasted_iota(jnp.int32, (P, P), 0)
    p_s = lax.broadcasted_iota(jnp.int32, (P, P), 1)
    same_chunk = lax.shift_right_logical(p_t, 6) == lax.shift_right_logical(p_s, 6)
    pair_mask = same_chunk & ((p_s >= p_t) if reverse else (p_s <= p_t))
    assert C == 64 and n_chunks % 2 == 0
    pair_rows = [slice(p * P, (p + 1) * P) for p in range(n_chunks // 2)]
    for p, rows in enumerate(pair_rows):
        for h, sl in enumerate(heads):
            a = _dot(qe_ref[rows, sl], ke_ref[rows, sl], ((1,), (1,)))
            a_ref[p, h] = jnp.where(pair_mask, a, 0.0).astype(BF16)
    for p, rows in enumerate(pair_rows):
        for h, sl in enumerate(heads):
            o_ref[rows, sl] = _dot(a_ref[p, h], v_ref[rows, sl])

    for c, rows in enumerate(chunk_rows):
        for h, sl in enumerate(heads):
            ut_ref[h, c] = _dot(v_ref[rows, sl], ku_ref[rows, sl], ((0,), (0,)))

    order = list(reversed(range(n_chunks))) if reverse else list(range(n_chunks))
    for h, sl in enumerate(heads):
        st = state_ref[h]
        for c in order:
            st_ref[h, c] = st.astype(BF16)
            st = st * dec_ref[c:c + 1, sl] + ut_ref[h, c]
        state_ref[h] = st

    for c, rows in enumerate(chunk_rows):
        for h, sl in enumerate(heads):
            o_ref[rows, sl] += _dot(qb_ref[rows, sl], st_ref[h, c], ((1,), (1,)))


def _hgrn(qs, f, v, batch, seq_len, reverse):
    T = qs.shape[0]
    tc = SEQ_TILE
    assert seq_len % tc == 0
    n_s = seq_len // tc
    if reverse:
        idx = lambda b, s: (b * n_s + (n_s - 1 - s), 0)
    else:
        idx = lambda b, s: (b * n_s + s, 0)
    spec = pl.BlockSpec((tc, REC_WIDTH), idx)
    return pl.pallas_call(
        functools.partial(_hgrn_kernel, reverse=reverse),
        grid=(batch, n_s),
        in_specs=[spec, spec, spec],
        out_specs=spec,
        out_shape=jax.ShapeDtypeStruct((T, REC_WIDTH), F32),
        scratch_shapes=[pltpu.VMEM((REC_HEADS, REC_DK, REC_DK), F32)]
        + [pltpu.VMEM((tc, REC_WIDTH), BF16)] * 4
        + [pltpu.VMEM((tc // REC_CHUNK, REC_WIDTH), F32),
           pltpu.VMEM((REC_HEADS, tc // REC_CHUNK, REC_DK, REC_DK), F32),
           pltpu.VMEM((REC_HEADS, tc // REC_CHUNK, REC_DK, REC_DK), BF16),
           pltpu.VMEM((tc, REC_WIDTH), F32),
           pltpu.VMEM((tc // (2 * REC_CHUNK), REC_HEADS, 2 * REC_CHUNK, 2 * REC_CHUNK), BF16)],
        compiler_params=_params(2),
        name="hgrn_bwd" if reverse else "hgrn_fwd",
    )(qs, f, v)


ATT_ROW_CHUNK = 32


def _attn_kernel(sink_ref, q_ref, kvp_ref, kvc_ref, kvn_ref, o_ref, s_ref, p_ref, bias_ref, *, seq_len):
    blk = pl.program_id(1)
    n_q, n_k = ATT_BLOCK, 3 * ATT_BLOCK
    qi = lax.broadcasted_iota(jnp.int32, (n_q, n_k), 0)
    kj = lax.broadcasted_iota(jnp.int32, (n_q, n_k), 1)
    kpos = blk * ATT_BLOCK - ATT_BLOCK + kj
    valid = (jnp.abs(kj - ATT_BLOCK - qi) <= WINDOW) & (kpos >= 0) & (kpos < seq_len)
    bias_ref[...] = jnp.where(valid, 0.0, -1e30)
    band = jnp.concatenate([kvp_ref[...], kvc_ref[...], kvn_ref[...]], axis=0)
    low = lax.broadcasted_iota(jnp.int32, (n_k, KV_WIDTH), 1) < ATT_HEAD_DIM
    zero = jnp.zeros((n_k, KV_WIDTH), BF16)
    k_plain, k_swap = band[:, 0:KV_WIDTH], band[:, KV_WIDTH:2 * KV_WIDTH]
    v_plain, v_swap = band[:, 2 * KV_WIDTH:3 * KV_WIDTH], band[:, 3 * KV_WIDTH:]
    slab = 2 * ATT_HEAD_DIM
    groups = range(ATT_KV_HEADS)
    for g in groups:
        src_lo, src_hi = (k_plain, k_swap) if g == 0 else (k_swap, k_plain)
        k_lo, k_hi = jnp.where(low, src_lo, zero), jnp.where(low, zero, src_hi)
        q2 = jnp.concatenate([q_ref[:, (2 * g) * slab:(2 * g + 1) * slab],
                              q_ref[:, (2 * g + 1) * slab:(2 * g + 2) * slab]], axis=0)
        s_ref[g, 0:2 * n_q, :] = _dot(q2, k_lo, ((1,), (1,)))
        s_ref[g, 2 * n_q:4 * n_q, :] = _dot(q2, k_hi, ((1,), (1,)))
    for g in groups:
        heads = (4 * g, 4 * g + 2, 4 * g + 1, 4 * g + 3)
        for r in range(4 * n_q // ATT_ROW_CHUNK):
            r0 = r * ATT_ROW_CHUNK
            q0 = r0 % n_q
            sink = sink_ref[heads[r0 // n_q]]
            rows, qrows = slice(r0, r0 + ATT_ROW_CHUNK), slice(q0, q0 + ATT_ROW_CHUNK)
            cols = [slice(j * ATT_BLOCK, (j + 1) * ATT_BLOCK) for j in range(3)]
            s = [s_ref[g, rows, cols[0]] + bias_ref[qrows, cols[0]],
                 s_ref[g, rows, cols[1]],
                 s_ref[g, rows, cols[2]] + bias_ref[qrows, cols[2]]]
            m = jnp.max(jnp.maximum(jnp.maximum(s[0], s[1]), s[2]), axis=-1, keepdims=True)
            m = jnp.maximum(m, sink)
            p = [jnp.exp(sj - m) for sj in s]
            den = jnp.sum(p[0] + p[1] + p[2], axis=-1, keepdims=True) + jnp.exp(sink - m)
            inv = 1.0 / den
            for j in range(3):
                p_ref[g, rows, cols[j]] = (p[j] * inv).astype(BF16)
    for g in groups:
        vsrc_lo, vsrc_hi = (v_plain, v_swap) if g == 0 else (v_swap, v_plain)
        v_lo, v_hi = jnp.where(low, vsrc_lo, zero), jnp.where(low, zero, vsrc_hi)
        for i in range(2):
            lo_rows = p_ref[g, i * n_q:(i + 1) * n_q, :]
            hi_rows = p_ref[g, (2 + i) * n_q:(3 + i) * n_q, :]
            o = _dot(lo_rows, v_lo) + _dot(hi_rows, v_hi)
            o_ref[:, (2 * g + i) * slab:(2 * g + i + 1) * slab] = o.astype(o_ref.dtype)


def _attn(qa, kv, sink, batch, seq_len):
    T = qa.shape[0]
    n_b = seq_len // ATT_BLOCK
    cur = lambda b, i: (b * n_b + i, 0)
    prev = lambda b, i: (b * n_b + jnp.maximum(i - 1, 0), 0)
    nxt = lambda b, i: (b * n_b + jnp.minimum(i + 1, n_b - 1), 0)
    kvs = lambda im: pl.BlockSpec((ATT_BLOCK, 4 * KV_WIDTH), im)
    n_rows = ATT_GROUP * ATT_BLOCK
    return pl.pallas_call(
        functools.partial(_attn_kernel, seq_len=seq_len),
        grid=(batch, n_b),
        in_specs=[pl.BlockSpec(memory_space=pltpu.SMEM),
                  pl.BlockSpec((ATT_BLOCK, ATT_WIDTH), cur), kvs(prev), kvs(cur), kvs(nxt)],
        out_specs=pl.BlockSpec((ATT_BLOCK, ATT_WIDTH), cur),
        out_shape=jax.ShapeDtypeStruct((T, ATT_WIDTH), BF16),
        scratch_shapes=[pltpu.VMEM((ATT_KV_HEADS, n_rows, 3 * ATT_BLOCK), F32),
                        pltpu.VMEM((ATT_KV_HEADS, n_rows, 3 * ATT_BLOCK), BF16),
                        pltpu.VMEM((ATT_BLOCK, 3 * ATT_BLOCK), F32)],
        compiler_params=_params(2),
        name="attn",
    )(sink, qa, kv, kv, kv)


def _mix_kernel(of_ref, ob_ref, gs_ref, att_ref, x0_ref, ng_ref, wo_ref,
                g1_ref, b1_ref, wr_ref, x1_ref, x1b_ref, aff_ref):
    tm = x0_ref.shape[0]
    halves = [slice(k * tm // 4, (k + 1) * tm // 4) for k in range(4)]
    ng = ng_ref[...]

    def gated_rec(rows):
        o_rec = of_ref[rows, :] + ob_ref[rows, :]
        gs = gs_ref[rows, :]
        recs = []
        for h in range(REC_HEADS):
            sl = slice(h * REC_DK, (h + 1) * REC_DK)
            oh = o_rec[:, sl]
            ms = jnp.mean(oh * oh, axis=-1, keepdims=True)
            recs.append(oh * lax.rsqrt(ms + NORM_EPS) * ng * gs[:, sl])
        return jnp.concatenate(recs, axis=1).astype(BF16)

    recs = [gated_rec(rows) for rows in halves]
    mixes = [_dot(rec, wo_ref[0:REC_WIDTH, :]) + _dot(att_ref[rows, :], wo_ref[REC_WIDTH:, :])
             for rows, rec in zip(halves, recs)]
    x1_parts = []
    for rows, mix in zip(halves, mixes):
        x1 = _layer_norm(DEEPNORM_ALPHA * x0_ref[rows, :] + mix, g1_ref[...], b1_ref[...])
        x1_ref[rows, :] = x1
        hi = x1.astype(BF16)
        x1b_ref[rows, :] = hi
        x1_parts.append((hi, (x1 - hi.astype(F32)).astype(BF16)))
    E = N_EXPERTS
    logits = []
    for hi, lo in x1_parts:
        a = _dot(wr_ref[...], hi, ((1,), (1,)))
        b = _dot(wr_ref[0:2 * E, :], lo, ((1,), (1,)))
        logits.append(a[0:E] + a[E:2 * E] + a[2 * E:] + b[0:E] + b[E:])
    for rows, lg in zip(halves, logits):
        m = jnp.max(lg, axis=0, keepdims=True)
        e = jnp.exp(lg - m)
        aff_ref[:, rows] = e / jnp.sum(e, axis=0, keepdims=True)


def _mix(of, ob, gs, att, x0, ng, w_out_bf16, g1, b1, w_router_t):
    T = x0.shape[0]
    tm = TOKEN_TILE
    row = lambda i: (i, 0)
    wide = lambda w: pl.BlockSpec((tm, w), row)
    vec = _resident((1, D_MODEL))
    return pl.pallas_call(
        _mix_kernel,
        grid=(T // tm,),
        in_specs=[wide(REC_WIDTH), wide(REC_WIDTH), wide(REC_WIDTH), wide(ATT_WIDTH), wide(D_MODEL),
                  _resident((1, REC_DK)), _resident((D_MODEL, D_MODEL)), vec, vec,
                  _resident((3 * N_EXPERTS, D_MODEL))],
        out_specs=[wide(D_MODEL), wide(D_MODEL), pl.BlockSpec((N_EXPERTS, tm), lambda i: (0, i))],
        out_shape=(jax.ShapeDtypeStruct((T, D_MODEL), F32),
                   jax.ShapeDtypeStruct((T, D_MODEL), BF16),
                   jax.ShapeDtypeStruct((N_EXPERTS, T), F32)),
        compiler_params=_params(1),
        name="mix",
    )(of, ob, gs, att, x0, ng, w_out_bf16, g1, b1, w_router_t)


def _select_kernel(aff_ref, pos_ref, cs_ref, *, cap, slot_offset):
    aff = aff_ref[0]
    nb, w = aff.shape
    bits = pltpu.bitcast(aff, jnp.int32)
    capf = jnp.float32(cap)

    def total(mask):
        x = mask.astype(F32)
        return jnp.sum(jnp.sum(x, axis=0, keepdims=True), axis=1, keepdims=True)

    def search(i, prefix):
        cand = prefix | lax.shift_left(jnp.int32(1), 30 - i)
        return jnp.where(total(bits >= cand) >= capf, cand, prefix)

    thr = lax.fori_loop(0, 31, search, jnp.zeros((1, 1), jnp.int32))

    c_i = lax.broadcasted_iota(jnp.int32, (w, w), 0)
    c_j = lax.broadcasted_iota(jnp.int32, (w, w), 1)
    upper = (c_i <= c_j).astype(BF16)
    r_i = lax.broadcasted_iota(jnp.int32, (nb, nb), 0)
    r_j = lax.broadcasted_iota(jnp.int32, (nb, nb), 1)
    strict_lower = (r_j < r_i).astype(BF16)

    def prefix_counts(mask):
        x = mask.astype(F32)
        inc = _dot(x.astype(BF16), upper)
        tot = jnp.broadcast_to(inc[:, w - 1:w], (nb, 128)).astype(BF16)
        before = _dot(strict_lower, tot)
        return inc - x + before[:, 0:1], before

    gt = bits > thr
    eq = bits == thr
    need = capf - total(gt)
    eq_rank, _ = prefix_counts(eq)
    sel = gt | (eq & (eq_rank < need))
    rank, before = prefix_counts(sel)
    pos_ref[0] = jnp.where(sel, rank.astype(jnp.int32) + slot_offset, -1)
    cs_ref[0] = before.astype(jnp.int32) + slot_offset


def _select(aff_blocks, cap, slot_offset):
    n_e, nb, w = aff_blocks.shape
    spec = pl.BlockSpec((1, nb, w), lambda e: (e, 0, 0))
    return pl.pallas_call(
        functools.partial(_select_kernel, cap=cap, slot_offset=slot_offset),
        grid=(n_e,),
        in_specs=[spec],
        out_specs=[spec, pl.BlockSpec((1, nb, 128), lambda e: (e, 0, 0))],
        out_shape=(jax.ShapeDtypeStruct((n_e, nb, w), jnp.int32),
                   jax.ShapeDtypeStruct((n_e, nb, 128), jnp.int32)),
        compiler_params=_params(1),
        name="select",
    )(aff_blocks)


def _dispatch_kernel(c_ref, xa_ref, xb_ref, pos_ref, xd_hbm, stage, onehot_ref, outbuf, sem, nflushed,
                     *, nb_first, nb_tot):
    i = pl.program_id(0)
    R, W = DISPATCH_TILE, COMBINE_WINDOW
    stride = nb_tot + 1
    group = 4

    @pl.when(i == 0)
    def _():
        stage[...] = jnp.zeros_like(stage)
        nflushed[0] = 0

    x = jnp.where(i < nb_first, xa_ref[...], xb_ref[...])
    slot_iota = lax.broadcasted_iota(jnp.int32, (W, R), 0)

    def out_copy(slot, e, row):
        return pltpu.make_async_copy(outbuf.at[slot], xd_hbm.at[e, pl.ds(row, R), :], sem.at[slot])

    def onehot(e, first):
        return jnp.where(pos_ref[0, e:e + 1, :] == slot_iota + first, 1.0, 0.0).astype(BF16)

    experts = range(N_EXPERTS)
    c0s = [c_ref[e * stride + i] for e in experts]
    c1s = [c_ref[e * stride + i + 1] for e in experts]
    bases = [(c0 // R) * R for c0 in c0s]
    w0s = [(c0 // WINDOW_ALIGN) * WINDOW_ALIGN for c0 in c0s]

    for e in experts:
        onehot_ref[e * W:(e + 1) * W, :] = onehot(e, w0s[e])
    moved = [_dot(onehot_ref[g0 * W:(g0 + group) * W, :], x).astype(BF16)
             for g0 in range(0, N_EXPERTS, group)]
    for e in experts:
        off = pl.multiple_of(w0s[e] - bases[e], WINDOW_ALIGN)
        r0 = (e % group) * W
        stage[e, pl.ds(off, W), :] += moved[e // group][r0:r0 + W, :]

    for e in experts:
        def window(w, carry, e=e):
            first = w0s[e] + w * W
            o2 = pl.multiple_of(first - bases[e], WINDOW_ALIGN)
            stage[e, pl.ds(o2, W), :] += _dot(onehot(e, first), x).astype(BF16)
            return carry

        lax.fori_loop(1, jnp.where(c1s[e] > c0s[e], (c1s[e] - w0s[e] + W - 1) // W, 0), window, 0)

    for e in experts:
        def flush(k, carry, e=e):
            f = nflushed[0]
            slot = f % 2

            @pl.when(f >= 2)
            def _():
                out_copy(slot, 0, 0).wait()

            outbuf[slot] = stage[e, 0:R, :]
            out_copy(slot, e, pl.multiple_of((c0s[e] // R + k) * R, R)).start()
            stage[e, 0:R, :] = stage[e, R:2 * R, :]
            stage[e, R:2 * R, :] = jnp.zeros((R, D_MODEL), BF16)
            nflushed[0] = f + 1
            return carry

        lax.fori_loop(0, c1s[e] // R - c0s[e] // R, flush, 0)

    @pl.when(i == nb_tot - 1)
    def _():
        out_copy(0, 0, 0).wait()
        out_copy(1, 0, 0).wait()


def _dispatch(cs_ext_flat, x1b_a, x1b_b, pos_blocks, cap_tot):
    nb_tot, n_e, R = pos_blocks.shape
    W = COMBINE_WINDOW
    nb_first = x1b_a.shape[0] // R
    assert n_e * (cap_tot // R) >= 2
    grid_spec = pltpu.PrefetchScalarGridSpec(
        num_scalar_prefetch=1,
        grid=(nb_tot,),
        in_specs=[pl.BlockSpec((R, D_MODEL), lambda i, *_: (jnp.minimum(i, nb_first - 1), 0)),
                  pl.BlockSpec((R, D_MODEL), lambda i, *_: (jnp.maximum(i - nb_first, 0), 0)),
                  pl.BlockSpec((1, n_e, R), lambda i, *_: (i, 0, 0))],
        out_specs=pl.BlockSpec(memory_space=pl.ANY),
        scratch_shapes=[pltpu.VMEM((n_e, 2 * R + W, D_MODEL), BF16),
                        pltpu.VMEM((n_e * W, R), BF16),
                        pltpu.VMEM((2, R, D_MODEL), BF16),
                        pltpu.SemaphoreType.DMA((2,)),
                        pltpu.SMEM((1,), jnp.int32)],
    )
    return pl.pallas_call(
        functools.partial(_dispatch_kernel, nb_first=nb_first, nb_tot=nb_tot),
        grid_spec=grid_spec,
        out_shape=jax.ShapeDtypeStruct((n_e, cap_tot, D_MODEL), BF16),
        compiler_params=_params(1),
        name="dispatch",
    )(cs_ext_flat, x1b_a, x1b_b, pos_blocks)


def _expert_kernel(x_ref, wg_ref, wu_ref, wd_ref, y_ref):
    xg = x_ref[0]
    y = jnp.zeros((xg.shape[0], D_MODEL), F32)
    for c in range(EXPERT_FF // FF_CHUNK):
        cs = slice(c * FF_CHUNK, (c + 1) * FF_CHUNK)
        h1 = _dot(xg, wg_ref[0, :, cs])
        h2 = _dot(xg, wu_ref[0, :, cs])
        h = (h1 * jax.nn.sigmoid(h1) * h2).astype(BF16)
        y = y + _dot(h, wd_ref[0, cs, :])
    y_ref[0] = y.astype(y_ref.dtype)


def _experts(xd, wg, wu, wd):
    n_e, cap_tot, _ = xd.shape
    tm = EXPERT_TILE
    assert cap_tot % tm == 0
    rows = pl.BlockSpec((1, tm, D_MODEL), lambda e, j: (e, j, 0))
    return pl.pallas_call(
        _expert_kernel,
        grid=(n_e, cap_tot // tm),
        in_specs=[rows,
                  pl.BlockSpec((1, D_MODEL, EXPERT_FF), lambda e, j: (e, 0, 0)),
                  pl.BlockSpec((1, D_MODEL, EXPERT_FF), lambda e, j: (e, 0, 0)),
                  pl.BlockSpec((1, EXPERT_FF, D_MODEL), lambda e, j: (e, 0, 0))],
        out_specs=rows,
        out_shape=jax.ShapeDtypeStruct((n_e, cap_tot, D_MODEL), BF16),
        compiler_params=_params(2),
        name="experts",
    )(xd, wg, wu, wd)


def _combine_kernel(w0_ref, nwin_ref, x1_ref, pos_ref, aff_ref, y_hbm, g2_ref, b2_ref, o_ref,
                    ymain, yextra, sem_main, sem_extra, gate_ref, acc_ref, *, n_blocks, cap_tot):
    i = pl.program_id(0)
    R, W = DISPATCH_TILE, COMBINE_WINDOW
    half = i % 2

    def main_copy(e, blk, h):
        w0 = pl.multiple_of(w0_ref[e * n_blocks + blk], WINDOW_ALIGN)
        return pltpu.make_async_copy(y_hbm.at[e, pl.ds(w0, W), :], ymain.at[h, pl.ds(e * W, W), :],
                                     sem_main.at[h, e])

    def start_main(blk, h):
        for e in range(N_EXPERTS):
            @pl.when(nwin_ref[e * n_blocks + blk] > 0)
            def _(e=e):
                main_copy(e, blk, h).start()

    @pl.when(i == 0)
    def _():
        ymain[...] = jnp.zeros_like(ymain)
        start_main(0, 0)

    @pl.when(i + 1 < n_blocks)
    def _():
        start_main(i + 1, 1 - half)

    slot_iota = lax.broadcasted_iota(jnp.int32, (W, R), 0)

    for e in range(N_EXPERTS):
        slot_ids = slot_iota + w0_ref[e * n_blocks + i]
        gate_ref[e * W:(e + 1) * W, :] = jnp.where(pos_ref[0, e:e + 1, :] == slot_ids,
                                                   aff_ref[0, e:e + 1, :], 0.0).astype(BF16)

    for e in range(N_EXPERTS):
        @pl.when(nwin_ref[e * n_blocks + i] > 0)
        def _(e=e):
            main_copy(e, i, half).wait()

    acc_ref[...] = DEEPNORM_ALPHA * x1_ref[...] + _dot(gate_ref[...], ymain[half], ((0,), (0,)))

    def expert_extras(e, carry):
        w0 = w0_ref[e * n_blocks + i]

        def window(w, c):
            first = w0 + w * W
            ws = pl.multiple_of(jnp.minimum(first, cap_tot - W), WINDOW_ALIGN)
            cp = pltpu.make_async_copy(y_hbm.at[e, pl.ds(ws, W), :], yextra, sem_extra)
            cp.start()
            cp.wait()
            slot_ids = slot_iota + ws
            hit = (pos_ref[0, pl.ds(e, 1), :] == slot_ids) & (slot_ids >= first)
            gated = jnp.where(hit, aff_ref[0, pl.ds(e, 1), :], 0.0).astype(BF16)
            acc_ref[...] += _dot(gated, yextra[...], ((0,), (0,)))
            return c

        return lax.fori_loop(1, nwin_ref[e * n_blocks + i], window, carry)

    lax.fori_loop(0, N_EXPERTS, expert_extras, 0)
    o_ref[...] = _layer_norm(acc_ref[...], g2_ref[...], b2_ref[...])


def _combine(w0, nwin, x1, pos_blocks, aff_blocks, y_all, g2, b2, block_offset):
    T = x1.shape[0]
    R, W = DISPATCH_TILE, COMBINE_WINDOW
    n_blocks = T // R
    cap_tot = y_all.shape[1]
    grid_spec = pltpu.PrefetchScalarGridSpec(
        num_scalar_prefetch=2,
        grid=(n_blocks,),
        in_specs=[pl.BlockSpec((R, D_MODEL), lambda i, *_: (i, 0)),
                  pl.BlockSpec((1, N_EXPERTS, R), lambda i, *_: (i + block_offset, 0, 0)),
                  pl.BlockSpec((1, N_EXPERTS, R), lambda i, *_: (i + block_offset, 0, 0)),
                  pl.BlockSpec(memory_space=pl.ANY),
                  pl.BlockSpec((1, D_MODEL), lambda i, *_: (0, 0)),
                  pl.BlockSpec((1, D_MODEL), lambda i, *_: (0, 0))],
        out_specs=pl.BlockSpec((R, D_MODEL), lambda i, *_: (i, 0)),
        scratch_shapes=[pltpu.VMEM((2, N_EXPERTS * W, D_MODEL), BF16),
                        pltpu.VMEM((W, D_MODEL), BF16),
                        pltpu.SemaphoreType.DMA((2, N_EXPERTS)),
                        pltpu.SemaphoreType.DMA(()),
                        pltpu.VMEM((N_EXPERTS * W, R), BF16),
                        pltpu.VMEM((R, D_MODEL), F32)],
    )
    return pl.pallas_call(
        functools.partial(_combine_kernel, n_blocks=n_blocks, cap_tot=cap_tot),
        grid_spec=grid_spec,
        out_shape=jax.ShapeDtypeStruct((T, D_MODEL), F32),
        compiler_params=_params(1),
        name="combine",
    )(w0, nwin, x1, pos_blocks, aff_blocks, y_all, g2, b2)


def _rotary_tables(seq_len):
    half = ROT_DIM // 2
    inv = ROPE_THETA ** (-jnp.arange(half, dtype=F32) / half)
    ang = jnp.arange(seq_len, dtype=F32)[:, None] * inv[None, :]
    cos, sin = jnp.cos(ang), jnp.sin(ang)
    pad = ATT_HEAD_DIM - ROT_DIM
    ones = jnp.ones((seq_len, pad), F32)
    zeros = jnp.zeros((seq_len, pad + half), F32)
    rc = jnp.concatenate([cos, cos, ones], axis=1)
    rsa = jnp.concatenate([-sin, zeros], axis=1)
    rsb = jnp.concatenate([jnp.zeros((seq_len, half), F32), sin, zeros[:, :pad]], axis=1)
    two = lambda t: jnp.concatenate([t, t], axis=1)
    return two(rc), two(rsa), two(rsb)


def _dispatch_plan(cs_all, cap_tot):
    n_e, nb_tot = cs_all.shape
    cs_ext = jnp.concatenate([cs_all, jnp.full((n_e, 1), cap_tot, jnp.int32)], axis=1)
    W = COMBINE_WINDOW
    c0, c1 = cs_ext[:, :-1], cs_ext[:, 1:]
    w0 = jnp.minimum((c0 // WINDOW_ALIGN) * WINDOW_ALIGN, cap_tot - W)
    nwin = jnp.where(c1 > c0, (c1 - w0 + W - 1) // W, 0)
    return cs_ext.reshape(-1), w0, nwin


def _split3(w):
    hi = w.astype(BF16)
    r1 = w - hi.astype(F32)
    md = r1.astype(BF16)
    lo = (r1 - md.astype(F32)).astype(BF16)
    return jnp.concatenate([hi, md, lo], axis=0)


def _trunk_front(x, seq_tables, p):
    B, S, D = x.shape
    T = B * S
    x2d = x.reshape(T, D)
    rc, rsa, rsb = seq_tables
    qs, ff, fb, v, gs, qa, kv, x0 = _in_proj(x2d, p["g0"], p["b0"], p["w_in"], p["lbf"], p["lbb"],
                                             rc, rsa, rsb, S)
    of = _hgrn(qs, ff, v, B, S, reverse=False)
    ob = _hgrn(qs, fb, v, B, S, reverse=True)
    att = _attn(qa, kv, p["sink"], B, S)
    return _mix(of, ob, gs, att, x0, p["ng"], p["w_out"], p["g1"], p["b1"], p["w_router_t"])


def kernel(x_prompt, x_sample, emb_ln_g, emb_ln_b, w_in, hgrn_lb_fwd, hgrn_lb_bwd, hgrn_norm_g, attn_sink, w_out, ln1_g, ln1_b, w_router, w_gate, w_up, w_down, ln2_g, ln2_b):
    assert w_in.shape[0] == 1, "single-layer problem"
    R = DISPATCH_TILE
    p = dict(
        g0=emb_ln_g.reshape(1, D_MODEL), b0=emb_ln_b.reshape(1, D_MODEL),
        w_in=w_in[0].astype(BF16), lbf=hgrn_lb_fwd, lbb=hgrn_lb_bwd,
        ng=hgrn_norm_g[0].reshape(1, REC_DK), sink=attn_sink[0],
        w_out=w_out[0].astype(BF16), g1=ln1_g[0].reshape(1, D_MODEL), b1=ln1_b[0].reshape(1, D_MODEL),
        w_router_t=_split3(w_router[0].T),
    )
    g2, b2 = ln2_g[0].reshape(1, D_MODEL), ln2_b[0].reshape(1, D_MODEL)
    wg, wu, wd = w_gate[0].astype(BF16), w_up[0].astype(BF16), w_down[0].astype(BF16)

    groups = []
    slot_offset = 0
    for x in (x_prompt, x_sample):
        B, S, _ = x.shape
        T = B * S
        cap = CAPACITY_FACTOR * T // N_EXPERTS
        assert T % TOKEN_TILE == 0 and cap % R == 0
        x1, x1b, aff = _trunk_front(x, _rotary_tables(S), p)
        aff_blocks = aff.reshape(N_EXPERTS, T // R, R)
        pos, cs = _select(aff_blocks, cap, slot_offset)
        groups.append(dict(x=x, x1=x1, x1b=x1b, aff=aff_blocks, pos=pos, cs=cs[:, :, 0]))
        slot_offset += cap
    cap_tot = slot_offset

    pos_all = jnp.concatenate([g["pos"] for g in groups], axis=1)
    aff_all = jnp.concatenate([g["aff"] for g in groups], axis=1)
    cs_all = jnp.concatenate([g["cs"] for g in groups], axis=1)
    cs_ext_flat, w0, nwin = _dispatch_plan(cs_all, cap_tot)
    pos_blocks = jnp.transpose(pos_all, (1, 0, 2))
    aff_blocks = jnp.transpose(aff_all, (1, 0, 2))
    xd = _dispatch(cs_ext_flat, groups[0]["x1b"], groups[1]["x1b"], pos_blocks, cap_tot)
    y_all = _experts(xd, wg, wu, wd)
    outs = []
    block_offset = 0
    for g in groups:
        nb = g["x1"].shape[0] // R
        sl = slice(block_offset, block_offset + nb)
        outs.append(_combine(w0[:, sl].reshape(-1), nwin[:, sl].reshape(-1), g["x1"], pos_blocks, aff_blocks,
                             y_all, g2, b2, block_offset).reshape(g["x"].shape))
        block_offset += nb
    return tuple(outs)
```

```python
import functools

import jax
import jax.numpy as jnp
import numpy as np
from jax import lax
from jax.experimental import pallas as pl
from jax.experimental.pallas import tpu as pltpu

D_MODEL = 1024
REC_WIDTH = 512
ATT_WIDTH = 512
REC_HEADS = 4
REC_DK = 128
REC_CHUNK = 64
ATT_HEADS = 8
ATT_KV_HEADS = 2
ATT_HEAD_DIM = 64
ATT_GROUP = 4
WINDOW = 128
ATT_BLOCK = 128
ROT_DIM = 16
ROPE_THETA = 500000.0
N_EXPERTS = 16
EXPERT_FF = 2048
CAPACITY_FACTOR = 2
NORM_EPS = 1e-5
DEEPNORM_ALPHA = 2.0 ** 0.25
KV_WIDTH = ATT_KV_HEADS * ATT_HEAD_DIM
IN_PROJ_WIDTH = 5 * REC_WIDTH + ATT_WIDTH + 2 * KV_WIDTH

TOKEN_TILE = 512
SEQ_TILE = 512
DISPATCH_TILE = 256
COMBINE_WINDOW = 64
WINDOW_ALIGN = 16
EXPERT_TILE = 512
FF_CHUNK = 512
VMEM_LIMIT_BYTES = 56 * 1024 * 1024

F32 = jnp.float32
BF16 = jnp.bfloat16
HIGHEST = lax.Precision.HIGHEST


def _params(n_axes):
    return pltpu.CompilerParams(dimension_semantics=("arbitrary",) * n_axes,
                                vmem_limit_bytes=VMEM_LIMIT_BYTES)


def _resident(shape):
    nd = len(shape)
    return pl.BlockSpec(shape, lambda *_: (0,) * nd, pipeline_mode=pl.Buffered(1))


def _layer_norm(x, g, b):
    mu = jnp.mean(x, axis=-1, keepdims=True)
    xc = x - mu
    var = jnp.mean(xc * xc, axis=-1, keepdims=True)
    return xc * lax.rsqrt(var + NORM_EPS) * g + b


def _lower_bound(p):
    p0, p1 = p[0:1, :], p[1:2, :]
    m = jnp.maximum(p0, p1)
    e0, e1 = jnp.exp(p0 - m), jnp.exp(p1 - m)
    return e0 / (e0 + e1)


def _dot(a, b, dims=((1,), (0,)), precision=None):
    return lax.dot_general(a, b, (dims, ((), ())), preferred_element_type=F32, precision=precision)


def _in_proj_kernel(x_ref, g_ref, b_ref, w_ref, lbf_ref, lbb_ref, rc_ref, rsa_ref, rsb_ref,
                    qs_ref, ff_ref, fb_ref, v_ref, gs_ref, qa_ref, kv_ref, x0_ref):
    tm = x_ref.shape[0]
    halves = [slice(0, tm // 2), slice(tm // 2, tm)]
    xn = []
    for rows in halves:
        x0 = _layer_norm(x_ref[rows, :], g_ref[...], b_ref[...])
        x0_ref[rows, :] = x0
        xn.append(x0.astype(BF16))
    projs = [_dot(x, w_ref[...]) for x in xn]
    W = REC_WIDTH
    lbf = _lower_bound(lbf_ref[...])
    lbb = _lower_bound(lbb_ref[...])
    for rows, proj in zip(halves, projs):
        q_r, zf, zb = proj[:, 0:W], proj[:, W:2 * W], proj[:, 2 * W:3 * W]
        i_r, g_r = proj[:, 3 * W:4 * W], proj[:, 4 * W:5 * W]
        q_a = proj[:, 5 * W:5 * W + ATT_WIDTH]
        k_a = proj[:, 5 * W + ATT_WIDTH:5 * W + ATT_WIDTH + KV_WIDTH]
        v_a = proj[:, 5 * W + ATT_WIDTH + KV_WIDTH:]
        qs_ref[rows, :] = q_r * jax.nn.sigmoid(q_r)
        ff_ref[rows, :] = lbf + (1.0 - lbf) * jax.nn.sigmoid(zf)
        fb_ref[rows, :] = lbb + (1.0 - lbb) * jax.nn.sigmoid(zb)
        v_ref[rows, :] = i_r.astype(BF16)
        gs_ref[rows, :] = g_r * jax.nn.sigmoid(g_r)
        rc, rsa, rsb = rc_ref[rows, :], rsa_ref[rows, :], rsb_ref[rows, :]
        k_rot = (k_a * rc + pltpu.roll(k_a, KV_WIDTH - ROT_DIM // 2, 1) * rsa
                 + pltpu.roll(k_a, ROT_DIM // 2, 1) * rsb)
        half_kv = KV_WIDTH // 2
        kv_ref[rows, 0:KV_WIDTH] = k_rot.astype(BF16)
        kv_ref[rows, KV_WIDTH:2 * KV_WIDTH] = pltpu.roll(k_rot, half_kv, 1).astype(BF16)
        kv_ref[rows, 2 * KV_WIDTH:3 * KV_WIDTH] = v_a.astype(BF16)
        kv_ref[rows, 3 * KV_WIDTH:] = pltpu.roll(v_a, half_kv, 1).astype(BF16)
        n_rep = ATT_WIDTH // KV_WIDTH
        rc4 = jnp.concatenate([rc] * n_rep, axis=1)
        rsa4 = jnp.concatenate([rsa] * n_rep, axis=1)
        rsb4 = jnp.concatenate([rsb] * n_rep, axis=1)
        q_rot = (q_a * rc4 + pltpu.roll(q_a, ATT_WIDTH - ROT_DIM // 2, 1) * rsa4
                 + pltpu.roll(q_a, ROT_DIM // 2, 1) * rsb4)
        qa_ref[rows, :] = (q_rot * (ATT_HEAD_DIM ** -0.5)).astype(BF16)


def _in_proj(x2d, g, b, w_in_bf16, lbf, lbb, rc, rsa, rsb, seq_len):
    T = x2d.shape[0]
    tm = TOKEN_TILE
    assert T % tm == 0 and seq_len % tm == 0
    n_seq = seq_len // tm
    row = lambda i: (i, 0)
    pos_row = lambda i: (i % n_seq, 0)
    wide = lambda w: pl.BlockSpec((tm, w), row)
    out_shapes = (
        jax.ShapeDtypeStruct((T, REC_WIDTH), F32),
        jax.ShapeDtypeStruct((T, REC_WIDTH), F32),
        jax.ShapeDtypeStruct((T, REC_WIDTH), F32),
        jax.ShapeDtypeStruct((T, REC_WIDTH), BF16),
        jax.ShapeDtypeStruct((T, REC_WIDTH), F32),
        jax.ShapeDtypeStruct((T, ATT_WIDTH), BF16),
        jax.ShapeDtypeStruct((T, 4 * KV_WIDTH), BF16),
        jax.ShapeDtypeStruct((T, D_MODEL), F32),
    )
    return pl.pallas_call(
        _in_proj_kernel,
        grid=(T // tm,),
        in_specs=[wide(D_MODEL), _resident((1, D_MODEL)), _resident((1, D_MODEL)),
                  _resident((D_MODEL, IN_PROJ_WIDTH)), _resident((2, REC_WIDTH)), _resident((2, REC_WIDTH)),
                  pl.BlockSpec((tm, KV_WIDTH), pos_row), pl.BlockSpec((tm, KV_WIDTH), pos_row),
                  pl.BlockSpec((tm, KV_WIDTH), pos_row)],
        out_specs=[wide(REC_WIDTH)] * 5 + [wide(ATT_WIDTH), wide(4 * KV_WIDTH), wide(D_MODEL)],
        out_shape=out_shapes,
        compiler_params=_params(1),
        name="in_proj",
    )(x2d, g, b, w_in_bf16, lbf, lbb, rc, rsa, rsb)


def _hgrn_kernel(q_ref, f_ref, v_ref, o_ref, state_ref, qe_ref, ke_ref, ku_ref, qb_ref, dec_ref,
                 ut_ref, st_ref, b_ref, a_ref, *, reverse):
    C, W = REC_CHUNK, REC_WIDTH
    n_chunks = q_ref.shape[0] // C
    heads = [slice(h * REC_DK, (h + 1) * REC_DK) for h in range(REC_HEADS)]
    chunk_rows = [slice(c * C, (c + 1) * C) for c in range(n_chunks)]

    @pl.when(pl.program_id(1) == 0)
    def _():
        state_ref[...] = jnp.zeros_like(state_ref)

    t_i = lax.broadcasted_iota(jnp.int32, (C, C), 0)
    s_i = lax.broadcasted_iota(jnp.int32, (C, C), 1)
    csum = jnp.where((s_i >= t_i) if reverse else (s_i <= t_i), 1.0, 0.0).astype(BF16)
    mid = C // 2 if reverse else C // 2 - 1
    last = 0 if reverse else C - 1

    for rows in chunk_rows:
        lf = jnp.log(f_ref[rows, :])
        hi = lf.astype(BF16)
        r1 = lf - hi.astype(F32)
        md = r1.astype(BF16)
        lo = (r1 - md.astype(F32)).astype(BF16)
        b3 = _dot(csum, jnp.concatenate([hi, md, lo], axis=1))
        b_ref[rows, :] = b3[:, 0:W] + b3[:, W:2 * W] + b3[:, 2 * W:]
    for c, rows in enumerate(chunk_rows):
        f = f_ref[rows, :]
        q = q_ref[rows, :]
        b = b_ref[rows, :]
        b_mid = b[mid:mid + 1, :]
        b_last = b[last:last + 1, :]
        kk = 1.0 - f
        qe = q * jnp.exp(b - b_mid)
        ke = kk * jnp.exp(b_mid - b)
        qe_ref[rows, :] = qe.astype(BF16)
        ke_ref[rows, :] = ke.astype(BF16)
        qb_ref[rows, :] = (qe * jnp.exp(b_mid)).astype(BF16)
        ku_ref[rows, :] = (ke * jnp.exp(b_last - b_mid)).astype(BF16)
        dec_ref[c:c + 1, :] = jnp.exp(b_last)

    P = 2 * C
    p_t = lax.broadcasted_iota(jnp.int32, (P, P), 0)
    p_s = lax.broadcasted_iota(jnp.int32, (P, P), 1)
    same_chunk = lax.shift_right_logical(p_t, 6) == lax.shift_right_logical(p_s, 6)
    pair_mask = same_chunk & ((p_s >= p_t) if reverse else (p_s <= p_t))
    assert C == 64 and n_chunks % 2 == 0
    pair_rows = [slice(p * P, (p + 1) * P) for p in range(n_chunks // 2)]
    for p, rows in enumerate(pair_rows):
        for h, sl in enumerate(heads):
            a = _dot(qe_ref[rows, sl], ke_ref[rows, sl], ((1,), (1,)))
            a_ref[p, h] = jnp.where(pair_mask, a, 0.0).astype(BF16)
    for p, rows in enumerate(pair_rows):
        for h, sl in enumerate(heads):
            o_ref[rows, sl] = _dot(a_ref[p, h], v_ref[rows, sl])

    for c, rows in enumerate(chunk_rows):
        for h, sl in enumerate(heads):
            ut_ref[h, c] = _dot(v_ref[rows, sl], ku_ref[rows, sl], ((0,), (0,)))

    order = list(reversed(range(n_chunks))) if reverse else list(range(n_chunks))
    for h, sl in enumerate(heads):
        st = state_ref[h]
        for c in order:
            st_ref[h, c] = st.astype(BF16)
            st = st * dec_ref[c:c + 1, sl] + ut_ref[h, c]
        state_ref[h] = st

    for c, rows in enumerate(chunk_rows):
        for h, sl in enumerate(heads):
            o_ref[rows, sl] += _dot(qb_ref[rows, sl], st_ref[h, c], ((1,), (1,)))


def _hgrn(qs, f, v, batch, seq_len, reverse):
    T = qs.shape[0]
    tc = SEQ_TILE
    assert seq_len % tc == 0
    n_s = seq_len // tc
    if reverse:
        idx = lambda b, s: (b * n_s + (n_s - 1 - s), 0)
    else:
        idx = lambda b, s: (b * n_s + s, 0)
    spec = pl.BlockSpec((tc, REC_WIDTH), idx)
    return pl.pallas_call(
        functools.partial(_hgrn_kernel, reverse=reverse),
        grid=(batch, n_s),
        in_specs=[spec, spec, spec],
        out_specs=spec,
        out_shape=jax.ShapeDtypeStruct((T, REC_WIDTH), F32),
        scratch_shapes=[pltpu.VMEM((REC_HEADS, REC_DK, REC_DK), F32)]
        + [pltpu.VMEM((tc, REC_WIDTH), BF16)] * 4
        + [pltpu.VMEM((tc // REC_CHUNK, REC_WIDTH), F32),
           pltpu.VMEM((REC_HEADS, tc // REC_CHUNK, REC_DK, REC_DK), F32),
           pltpu.VMEM((REC_HEADS, tc // REC_CHUNK, REC_DK, REC_DK), BF16),
           pltpu.VMEM((tc, REC_WIDTH), F32),
           pltpu.VMEM((tc // (2 * REC_CHUNK), REC_HEADS, 2 * REC_CHUNK, 2 * REC_CHUNK), BF16)],
        compiler_params=_params(2),
        name="hgrn_bwd" if reverse else "hgrn_fwd",
    )(qs, f, v)


ATT_ROW_CHUNK = 32


def _attn_kernel(sink_ref, q_ref, kvp_ref, kvc_ref, kvn_ref, o_ref, s_ref, p_ref, bias_ref, *, seq_len):
    blk = pl.program_id(1)
    n_q, n_k = ATT_BLOCK, 3 * ATT_BLOCK
    qi = lax.broadcasted_iota(jnp.int32, (n_q, n_k), 0)
    kj = lax.broadcasted_iota(jnp.int32, (n_q, n_k), 1)
    kpos = blk * ATT_BLOCK - ATT_BLOCK + kj
    valid = (jnp.abs(kj - ATT_BLOCK - qi) <= WINDOW) & (kpos >= 0) & (kpos < seq_len)
    bias_ref[...] = jnp.where(valid, 0.0, -1e30)
    band = jnp.concatenate([kvp_ref[...], kvc_ref[...], kvn_ref[...]], axis=0)
    low = lax.broadcasted_iota(jnp.int32, (n_k, KV_WIDTH), 1) < ATT_HEAD_DIM
    zero = jnp.zeros((n_k, KV_WIDTH), BF16)
    k_plain, k_swap = band[:, 0:KV_WIDTH], band[:, KV_WIDTH:2 * KV_WIDTH]
    v_plain, v_swap = band[:, 2 * KV_WIDTH:3 * KV_WIDTH], band[:, 3 * KV_WIDTH:]
    slab = 2 * ATT_HEAD_DIM
    groups = range(ATT_KV_HEADS)
    for g in groups:
        src_lo, src_hi = (k_plain, k_swap) if g == 0 else (k_swap, k_plain)
        k_lo, k_hi = jnp.where(low, src_lo, zero), jnp.where(low, zero, src_hi)
        q2 = jnp.concatenate([q_ref[:, (2 * g) * slab:(2 * g + 1) * slab],
                              q_ref[:, (2 * g + 1) * slab:(2 * g + 2) * slab]], axis=0)
        s_ref[g, 0:2 * n_q, :] = _dot(q2, k_lo, ((1,), (1,)))
        s_ref[g, 2 * n_q:4 * n_q, :] = _dot(q2, k_hi, ((1,), (1,)))
    for g in groups:
        heads = (4 * g, 4 * g + 2, 4 * g + 1, 4 * g + 3)
        for r in range(4 * n_q // ATT_ROW_CHUNK):
            r0 = r * ATT_ROW_CHUNK
            q0 = r0 % n_q
            sink = sink_ref[heads[r0 // n_q]]
            rows, qrows = slice(r0, r0 + ATT_ROW_CHUNK), slice(q0, q0 + ATT_ROW_CHUNK)
            cols = [slice(j * ATT_BLOCK, (j + 1) * ATT_BLOCK) for j in range(3)]
            s = [s_ref[g, rows, cols[0]] + bias_ref[qrows, cols[0]],
                 s_ref[g, rows, cols[1]],
                 s_ref[g, rows, cols[2]] + bias_ref[qrows, cols[2]]]
            m = jnp.max(jnp.maximum(jnp.maximum(s[0], s[1]), s[2]), axis=-1, keepdims=True)
            m = jnp.maximum(m, sink)
            p = [jnp.exp(sj - m) for sj in s]
            den = jnp.sum(p[0] + p[1] + p[2], axis=-1, keepdims=True) + jnp.exp(sink - m)
            inv = 1.0 / den
            for j in range(3):
                p_ref[g, rows, cols[j]] = (p[j] * inv).astype(BF16)
    for g in groups:
        vsrc_lo, vsrc_hi = (v_plain, v_swap) if g == 0 else (v_swap, v_plain)
        v_lo, v_hi = jnp.where(low, vsrc_lo, zero), jnp.where(low, zero, vsrc_hi)
        for i in range(2):
            lo_rows = p_ref[g, i * n_q:(i + 1) * n_q, :]
            hi_rows = p_ref[g, (2 + i) * n_q:(3 + i) * n_q, :]
            o = _dot(lo_rows, v_lo) + _dot(hi_rows, v_hi)
            o_ref[:, (2 * g + i) * slab:(2 * g + i + 1) * slab] = o.astype(o_ref.dtype)


def _attn(qa, kv, sink, batch, seq_len):
    T = qa.shape[0]
    n_b = seq_len // ATT_BLOCK
    cur = lambda b, i: (b * n_b + i, 0)
    prev = lambda b, i: (b * n_b + jnp.maximum(i - 1, 0), 0)
    nxt = lambda b, i: (b * n_b + jnp.minimum(i + 1, n_b - 1), 0)
    kvs = lambda im: pl.BlockSpec((ATT_BLOCK, 4 * KV_WIDTH), im)
    n_rows = ATT_GROUP * ATT_BLOCK
    return pl.pallas_call(
        functools.partial(_attn_kernel, seq_len=seq_len),
        grid=(batch, n_b),
        in_specs=[pl.BlockSpec(memory_space=pltpu.SMEM),
                  pl.BlockSpec((ATT_BLOCK, ATT_WIDTH), cur), kvs(prev), kvs(cur), kvs(nxt)],
        out_specs=pl.BlockSpec((ATT_BLOCK, ATT_WIDTH), cur),
        out_shape=jax.ShapeDtypeStruct((T, ATT_WIDTH), BF16),
        scratch_shapes=[pltpu.VMEM((ATT_KV_HEADS, n_rows, 3 * ATT_BLOCK), F32),
                        pltpu.VMEM((ATT_KV_HEADS, n_rows, 3 * ATT_BLOCK), BF16),
                        pltpu.VMEM((ATT_BLOCK, 3 * ATT_BLOCK), F32)],
        compiler_params=_params(2),
        name="attn",
    )(sink, qa, kv, kv, kv)


def _mix_kernel(of_ref, ob_ref, gs_ref, att_ref, x0_ref, ng_ref, wo_ref,
                g1_ref, b1_ref, wr_ref, x1_ref, x1b_ref, aff_ref):
    tm = x0_ref.shape[0]
    halves = [slice(k * tm // 4, (k + 1) * tm // 4) for k in range(4)]
    ng = ng_ref[...]

    def gated_rec(rows):
        o_rec = of_ref[rows, :] + ob_ref[rows, :]
        gs = gs_ref[rows, :]
        recs = []
        for h in range(REC_HEADS):
            sl = slice(h * REC_DK, (h + 1) * REC_DK)
            oh = o_rec[:, sl]
            ms = jnp.mean(oh * oh, axis=-1, keepdims=True)
            recs.append(oh * lax.rsqrt(ms + NORM_EPS) * ng * gs[:, sl])
        return jnp.concatenate(recs, axis=1).astype(BF16)

    recs = [gated_rec(rows) for rows in halves]
    mixes = [_dot(rec, wo_ref[0:REC_WIDTH, :]) + _dot(att_ref[rows, :], wo_ref[REC_WIDTH:, :])
             for rows, rec in zip(halves, recs)]
    x1_parts = []
    for rows, mix in zip(halves, mixes):
        x1 = _layer_norm(DEEPNORM_ALPHA * x0_ref[rows, :] + mix, g1_ref[...], b1_ref[...])
        x1_ref[rows, :] = x1
        hi = x1.astype(BF16)
        x1b_ref[rows, :] = hi
        x1_parts.append((hi, (x1 - hi.astype(F32)).astype(BF16)))
    E = N_EXPERTS
    logits = []
    for hi, lo in x1_parts:
        a = _dot(wr_ref[...], hi, ((1,), (1,)))
        b = _dot(wr_ref[0:2 * E, :], lo, ((1,), (1,)))
        logits.append(a[0:E] + a[E:2 * E] + a[2 * E:] + b[0:E] + b[E:])
    for rows, lg in zip(halves, logits):
        m = jnp.max(lg, axis=0, keepdims=True)
        e = jnp.exp(lg - m)
        aff_ref[:, rows] = e / jnp.sum(e, axis=0, keepdims=True)


def _mix(of, ob, gs, att, x0, ng, w_out_bf16, g1, b1, w_router_t):
    T = x0.shape[0]
    tm = TOKEN_TILE
    row = lambda i: (i, 0)
    wide = lambda w: pl.BlockSpec((tm, w), row)
    vec = _resident((1, D_MODEL))
    return pl.pallas_call(
        _mix_kernel,
        grid=(T // tm,),
        in_specs=[wide(REC_WIDTH), wide(REC_WIDTH), wide(REC_WIDTH), wide(ATT_WIDTH), wide(D_MODEL),
                  _resident((1, REC_DK)), _resident((D_MODEL, D_MODEL)), vec, vec,
                  _resident((3 * N_EXPERTS, D_MODEL))],
        out_specs=[wide(D_MODEL), wide(D_MODEL), pl.BlockSpec((N_EXPERTS, tm), lambda i: (0, i))],
        out_shape=(jax.ShapeDtypeStruct((T, D_MODEL), F32),
                   jax.ShapeDtypeStruct((T, D_MODEL), BF16),
                   jax.ShapeDtypeStruct((N_EXPERTS, T), F32)),
        compiler_params=_params(1),
        name="mix",
    )(of, ob, gs, att, x0, ng, w_out_bf16, g1, b1, w_router_t)


def _select_kernel(aff_ref, pos_ref, cs_ref, *, cap, slot_offset):
    aff = aff_ref[0]
    nb, w = aff.shape
    capf = jnp.float32(cap)

    def total(mask):
        x = mask.astype(F32)
        return jnp.sum(jnp.sum(x, axis=0, keepdims=True), axis=1, keepdims=True)

    def search(i, prefix):
        cand = prefix | lax.shift_left(jnp.int32(1), 30 - i)
        return jnp.where(total(aff >= pltpu.bitcast(cand, F32)) >= capf, cand, prefix)

    thr = pltpu.bitcast(lax.fori_loop(0, 31, search, jnp.zeros((1, 1), jnp.int32)), F32)

    c_i = lax.broadcasted_iota(jnp.int32, (w, w), 0)
    c_j = lax.broadcasted_iota(jnp.int32, (w, w), 1)
    upper = (c_i <= c_j).astype(BF16)
    r_i = lax.broadcasted_iota(jnp.int32, (nb, nb), 0)
    r_j = lax.broadcasted_iota(jnp.int32, (nb, nb), 1)
    strict_lower = (r_j < r_i).astype(BF16)

    def prefix_counts(mask):
        x = mask.astype(F32)
        inc = _dot(x.astype(BF16), upper)
        tot = jnp.broadcast_to(inc[:, w - 1:w], (nb, 128)).astype(BF16)
        before = _dot(strict_lower, tot)
        return inc - x + before[:, 0:1], before

    gt = aff > thr
    eq = aff == thr
    need = capf - total(gt)
    eq_rank, _ = prefix_counts(eq)
    sel = gt | (eq & (eq_rank < need))
    rank, before = prefix_counts(sel)
    pos_ref[0] = jnp.where(sel, rank.astype(jnp.int32) + slot_offset, -1)
    cs_ref[0] = before.astype(jnp.int32) + slot_offset


def _select(aff_blocks, cap, slot_offset):
    n_e, nb, w = aff_blocks.shape
    spec = pl.BlockSpec((1, nb, w), lambda e: (e, 0, 0))
    return pl.pallas_call(
        functools.partial(_select_kernel, cap=cap, slot_offset=slot_offset),
        grid=(n_e,),
        in_specs=[spec],
        out_specs=[spec, pl.BlockSpec((1, nb, 128), lambda e: (e, 0, 0))],
        out_shape=(jax.ShapeDtypeStruct((n_e, nb, w), jnp.int32),
                   jax.ShapeDtypeStruct((n_e, nb, 128), jnp.int32)),
        compiler_params=_params(1),
        name="select",
    )(aff_blocks)


def _dispatch_kernel(c_ref, xa_ref, xb_ref, pos_ref, xd_hbm, stage, onehot_ref, outbuf, sem, nflushed,
                     *, nb_first, nb_tot):
    i = pl.program_id(0)
    R, W = DISPATCH_TILE, COMBINE_WINDOW
    stride = nb_tot + 1
    group = 4

    @pl.when(i == 0)
    def _():
        stage[...] = jnp.zeros_like(stage)
        nflushed[0] = 0

    x = jnp.where(i < nb_first, xa_ref[...], xb_ref[...])
    slot_iota = lax.broadcasted_iota(jnp.int32, (W, R), 0)

    def out_copy(slot, e, row):
        return pltpu.make_async_copy(outbuf.at[slot], xd_hbm.at[e, pl.ds(row, R), :], sem.at[slot])

    def onehot(e, first):
        return jnp.where(pos_ref[0, e:e + 1, :] == slot_iota + first, 1.0, 0.0).astype(BF16)

    experts = range(N_EXPERTS)
    c0s = [c_ref[e * stride + i] for e in experts]
    c1s = [c_ref[e * stride + i + 1] for e in experts]
    bases = [(c0 // R) * R for c0 in c0s]
    w0s = [(c0 // WINDOW_ALIGN) * WINDOW_ALIGN for c0 in c0s]

    for e in experts:
        onehot_ref[e * W:(e + 1) * W, :] = onehot(e, w0s[e])
    moved = [_dot(onehot_ref[g0 * W:(g0 + group) * W, :], x).astype(BF16)
             for g0 in range(0, N_EXPERTS, group)]
    for e in experts:
        off = pl.multiple_of(w0s[e] - bases[e], WINDOW_ALIGN)
        r0 = (e % group) * W
        stage[e, pl.ds(off, W), :] += moved[e // group][r0:r0 + W, :]

    for e in experts:
        def window(w, carry, e=e):
            first = w0s[e] + w * W
            o2 = pl.multiple_of(first - bases[e], WINDOW_ALIGN)
            stage[e, pl.ds(o2, W), :] += _dot(onehot(e, first), x).astype(BF16)
            return carry

        lax.fori_loop(1, jnp.where(c1s[e] > c0s[e], (c1s[e] - w0s[e] + W - 1) // W, 0), window, 0)

    for e in experts:
        def flush(k, carry, e=e):
            f = nflushed[0]
            slot = f % 2

            @pl.when(f >= 2)
            def _():
                out_copy(slot, 0, 0).wait()

            outbuf[slot] = stage[e, 0:R, :]
            out_copy(slot, e, pl.multiple_of((c0s[e] // R + k) * R, R)).start()
            stage[e, 0:R, :] = stage[e, R:2 * R, :]
            stage[e, R:2 * R, :] = jnp.zeros((R, D_MODEL), BF16)
            nflushed[0] = f + 1
            return carry

        lax.fori_loop(0, c1s[e] // R - c0s[e] // R, flush, 0)

    @pl.when(i == nb_tot - 1)
    def _():
        out_copy(0, 0, 0).wait()
        out_copy(1, 0, 0).wait()


def _dispatch(cs_ext_flat, x1b_a, x1b_b, pos_blocks, cap_tot):
    nb_tot, n_e, R = pos_blocks.shape
    W = COMBINE_WINDOW
    nb_first = x1b_a.shape[0] // R
    assert n_e * (cap_tot // R) >= 2
    grid_spec = pltpu.PrefetchScalarGridSpec(
        num_scalar_prefetch=1,
        grid=(nb_tot,),
        in_specs=[pl.BlockSpec((R, D_MODEL), lambda i, *_: (jnp.minimum(i, nb_first - 1), 0)),
                  pl.BlockSpec((R, D_MODEL), lambda i, *_: (jnp.maximum(i - nb_first, 0), 0)),
                  pl.BlockSpec((1, n_e, R), lambda i, *_: (i, 0, 0))],
        out_specs=pl.BlockSpec(memory_space=pl.ANY),
        scratch_shapes=[pltpu.VMEM((n_e, 2 * R + W, D_MODEL), BF16),
                        pltpu.VMEM((n_e * W, R), BF16),
                        pltpu.VMEM((2, R, D_MODEL), BF16),
                        pltpu.SemaphoreType.DMA((2,)),
                        pltpu.SMEM((1,), jnp.int32)],
    )
    return pl.pallas_call(
        functools.partial(_dispatch_kernel, nb_first=nb_first, nb_tot=nb_tot),
        grid_spec=grid_spec,
        out_shape=jax.ShapeDtypeStruct((n_e, cap_tot, D_MODEL), BF16),
        compiler_params=_params(1),
        name="dispatch",
    )(cs_ext_flat, x1b_a, x1b_b, pos_blocks)


def _expert_kernel(x_ref, wg_ref, wu_ref, wd_ref, y_ref):
    xg = x_ref[0]
    y = jnp.zeros((xg.shape[0], D_MODEL), F32)
    for c in range(EXPERT_FF // FF_CHUNK):
        cs = slice(c * FF_CHUNK, (c + 1) * FF_CHUNK)
        h1 = _dot(xg, wg_ref[0, :, cs])
        h2 = _dot(xg, wu_ref[0, :, cs])
        h = (h1 * jax.nn.sigmoid(h1) * h2).astype(BF16)
        y = y + _dot(h, wd_ref[0, cs, :])
    y_ref[0] = y.astype(y_ref.dtype)


def _experts(xd, wg, wu, wd):
    n_e, cap_tot, _ = xd.shape
    tm = EXPERT_TILE
    assert cap_tot % tm == 0
    rows = pl.BlockSpec((1, tm, D_MODEL), lambda e, j: (e, j, 0))
    return pl.pallas_call(
        _expert_kernel,
        grid=(n_e, cap_tot // tm),
        in_specs=[rows,
                  pl.BlockSpec((1, D_MODEL, EXPERT_FF), lambda e, j: (e, 0, 0)),
                  pl.BlockSpec((1, D_MODEL, EXPERT_FF), lambda e, j: (e, 0, 0)),
                  pl.BlockSpec((1, EXPERT_FF, D_MODEL), lambda e, j: (e, 0, 0))],
        out_specs=rows,
        out_shape=jax.ShapeDtypeStruct((n_e, cap_tot, D_MODEL), BF16),
        compiler_params=_params(2),
        name="experts",
    )(xd, wg, wu, wd)


def _combine_kernel(w0_ref, nwin_ref, x1_ref, pos_ref, aff_ref, y_hbm, g2_ref, b2_ref, o_ref,
                    ymain, yextra, sem_main, sem_extra, gate_ref, acc_ref, *, n_blocks, cap_tot):
    i = pl.program_id(0)
    R, W = DISPATCH_TILE, COMBINE_WINDOW
    half = i % 2

    def main_copy(e, blk, h):
        w0 = pl.multiple_of(w0_ref[e * n_blocks + blk], WINDOW_ALIGN)
        return pltpu.make_async_copy(y_hbm.at[e, pl.ds(w0, W), :], ymain.at[h, pl.ds(e * W, W), :],
                                     sem_main.at[h, e])

    def start_main(blk, h):
        for e in range(N_EXPERTS):
            @pl.when(nwin_ref[e * n_blocks + blk] > 0)
            def _(e=e):
                main_copy(e, blk, h).start()

    @pl.when(i == 0)
    def _():
        ymain[...] = jnp.zeros_like(ymain)
        start_main(0, 0)

    @pl.when(i + 1 < n_blocks)
    def _():
        start_main(i + 1, 1 - half)

    slot_iota = lax.broadcasted_iota(jnp.int32, (W, R), 0)

    for e in range(N_EXPERTS):
        slot_ids = slot_iota + w0_ref[e * n_blocks + i]
        gate_ref[e * W:(e + 1) * W, :] = jnp.where(pos_ref[0, e:e + 1, :] == slot_ids,
                                                   aff_ref[0, e:e + 1, :], 0.0).astype(BF16)

    for e in range(N_EXPERTS):
        @pl.when(nwin_ref[e * n_blocks + i] > 0)
        def _(e=e):
            main_copy(e, i, half).wait()

    acc_ref[...] = DEEPNORM_ALPHA * x1_ref[...] + _dot(gate_ref[...], ymain[half], ((0,), (0,)))

    def expert_extras(e, carry):
        w0 = w0_ref[e * n_blocks + i]

        def window(w, c):
            first = w0 + w * W
            ws = pl.multiple_of(jnp.minimum(first, cap_tot - W), WINDOW_ALIGN)
            cp = pltpu.make_async_copy(y_hbm.at[e, pl.ds(ws, W), :], yextra, sem_extra)
            cp.start()
            cp.wait()
            slot_ids = slot_iota + ws
            hit = (pos_ref[0, pl.ds(e, 1), :] == slot_ids) & (slot_ids >= first)
            gated = jnp.where(hit, aff_ref[0, pl.ds(e, 1), :], 0.0).astype(BF16)
            acc_ref[...] += _dot(gated, yextra[...], ((0,), (0,)))
            return c

        return lax.fori_loop(1, nwin_ref[e * n_blocks + i], window, carry)

    lax.fori_loop(0, N_EXPERTS, expert_extras, 0)
    o_ref[...] = _layer_norm(acc_ref[...], g2_ref[...], b2_ref[...])


def _combine(w0, nwin, x1, pos_blocks, aff_blocks, y_all, g2, b2, block_offset):
    T = x1.shape[0]
    R, W = DISPATCH_TILE, COMBINE_WINDOW
    n_blocks = T // R
    cap_tot = y_all.shape[1]
    grid_spec = pltpu.PrefetchScalarGridSpec(
        num_scalar_prefetch=2,
        grid=(n_blocks,),
        in_specs=[pl.BlockSpec((R, D_MODEL), lambda i, *_: (i, 0)),
                  pl.BlockSpec((1, N_EXPERTS, R), lambda i, *_: (i + block_offset, 0, 0)),
                  pl.BlockSpec((1, N_EXPERTS, R), lambda i, *_: (i + block_offset, 0, 0)),
                  pl.BlockSpec(memory_space=pl.ANY),
                  pl.BlockSpec((1, D_MODEL), lambda i, *_: (0, 0)),
                  pl.BlockSpec((1, D_MODEL), lambda i, *_: (0, 0))],
        out_specs=pl.BlockSpec((R, D_MODEL), lambda i, *_: (i, 0)),
        scratch_shapes=[pltpu.VMEM((2, N_EXPERTS * W, D_MODEL), BF16),
                        pltpu.VMEM((W, D_MODEL), BF16),
                        pltpu.SemaphoreType.DMA((2, N_EXPERTS)),
                        pltpu.SemaphoreType.DMA(()),
                        pltpu.VMEM((N_EXPERTS * W, R), BF16),
                        pltpu.VMEM((R, D_MODEL), F32)],
    )
    return pl.pallas_call(
        functools.partial(_combine_kernel, n_blocks=n_blocks, cap_tot=cap_tot),
        grid_spec=grid_spec,
        out_shape=jax.ShapeDtypeStruct((T, D_MODEL), F32),
        compiler_params=_params(1),
        name="combine",
    )(w0, nwin, x1, pos_blocks, aff_blocks, y_all, g2, b2)


def _rotary_tables(seq_len):
    half = ROT_DIM // 2
    inv = ROPE_THETA ** (-jnp.arange(half, dtype=F32) / half)
    ang = jnp.arange(seq_len, dtype=F32)[:, None] * inv[None, :]
    cos, sin = jnp.cos(ang), jnp.sin(ang)
    pad = ATT_HEAD_DIM - ROT_DIM
    ones = jnp.ones((seq_len, pad), F32)
    zeros = jnp.zeros((seq_len, pad + half), F32)
    rc = jnp.concatenate([cos, cos, ones], axis=1)
    rsa = jnp.concatenate([-sin, zeros], axis=1)
    rsb = jnp.concatenate([jnp.zeros((seq_len, half), F32), sin, zeros[:, :pad]], axis=1)
    two = lambda t: jnp.concatenate([t, t], axis=1)
    return two(rc), two(rsa), two(rsb)


def _dispatch_plan(cs_all, cap_tot):
    n_e, nb_tot = cs_all.shape
    cs_ext = jnp.concatenate([cs_all, jnp.full((n_e, 1), cap_tot, jnp.int32)], axis=1)
    W = COMBINE_WINDOW
    c0, c1 = cs_ext[:, :-1], cs_ext[:, 1:]
    w0 = jnp.minimum((c0 // WINDOW_ALIGN) * WINDOW_ALIGN, cap_tot - W)
    nwin = jnp.where(c1 > c0, (c1 - w0 + W - 1) // W, 0)
    return cs_ext.reshape(-1), w0, nwin


def _split3(w):
    def head(t):
        bits = lax.bitcast_convert_type(t, jnp.uint32) & jnp.uint32(0xFFFF0000)
        return lax.bitcast_convert_type(bits, F32)

    hi = head(w)
    r1 = w - hi
    md = head(r1)
    lo = r1 - md
    return jnp.concatenate([hi, md, lo], axis=0).astype(BF16)


def _trunk_front(x, seq_tables, p):
    B, S, D = x.shape
    T = B * S
    x2d = x.reshape(T, D)
    rc, rsa, rsb = seq_tables
    qs, ff, fb, v, gs, qa, kv, x0 = _in_proj(x2d, p["g0"], p["b0"], p["w_in"], p["lbf"], p["lbb"],
                                             rc, rsa, rsb, S)
    of = _hgrn(qs, ff, v, B, S, reverse=False)
    ob = _hgrn(qs, fb, v, B, S, reverse=True)
    att = _attn(qa, kv, p["sink"], B, S)
    return _mix(of, ob, gs, att, x0, p["ng"], p["w_out"], p["g1"], p["b1"], p["w_router_t"])


def kernel(x_prompt, x_sample, emb_ln_g, emb_ln_b, w_in, hgrn_lb_fwd, hgrn_lb_bwd, hgrn_norm_g, attn_sink, w_out, ln1_g, ln1_b, w_router, w_gate, w_up, w_down, ln2_g, ln2_b):
    assert w_in.shape[0] == 1, "single-layer problem"
    R = DISPATCH_TILE
    p = dict(
        g0=emb_ln_g.reshape(1, D_MODEL), b0=emb_ln_b.reshape(1, D_MODEL),
        w_in=w_in[0].astype(BF16), lbf=hgrn_lb_fwd, lbb=hgrn_lb_bwd,
        ng=hgrn_norm_g[0].reshape(1, REC_DK), sink=attn_sink[0],
        w_out=w_out[0].astype(BF16), g1=ln1_g[0].reshape(1, D_MODEL), b1=ln1_b[0].reshape(1, D_MODEL),
        w_router_t=_split3(w_router[0].T),
    )
    g2, b2 = ln2_g[0].reshape(1, D_MODEL), ln2_b[0].reshape(1, D_MODEL)
    wg, wu, wd = w_gate[0].astype(BF16), w_up[0].astype(BF16), w_down[0].astype(BF16)

    groups = []
    slot_offset = 0
    for x in (x_prompt, x_sample):
        B, S, _ = x.shape
        T = B * S
        cap = CAPACITY_FACTOR * T // N_EXPERTS
        assert T % TOKEN_TILE == 0 and cap % R == 0
        x1, x1b, aff = _trunk_front(x, _rotary_tables(S), p)
        aff_blocks = aff.reshape(N_EXPERTS, T // R, R)
        pos, cs = _select(aff_blocks, cap, slot_offset)
        groups.append(dict(x=x, x1=x1, x1b=x1b, aff=aff_blocks, pos=pos, cs=cs[:, :, 0]))
        slot_offset += cap
    cap_tot = slot_offset

    pos_all = jnp.concatenate([g["pos"] for g in groups], axis=1)
    aff_all = jnp.concatenate([g["aff"] for g in groups], axis=1)
    cs_all = jnp.concatenate([g["cs"] for g in groups], axis=1)
    cs_ext_flat, w0, nwin = _dispatch_plan(cs_all, cap_tot)
    pos_blocks = jnp.transpose(pos_all, (1, 0, 2))
    aff_blocks = jnp.transpose(aff_all, (1, 0, 2))
    xd = _dispatch(cs_ext_flat, groups[0]["x1b"], groups[1]["x1b"], pos_blocks, cap_tot)
    y_all = _experts(xd, wg, wu, wd)
    outs = []
    block_offset = 0
    for g in groups:
        nb = g["x1"].shape[0] // R
        sl = slice(block_offset, block_offset + nb)
        outs.append(_combine(w0[:, sl].reshape(-1), nwin[:, sl].reshape(-1), g["x1"], pos_blocks, aff_blocks,
                             y_all, g2, b2, block_offset).reshape(g["x"].shape))
        block_offset += nb
    return tuple(outs)
```
